```python
import jax, jax.numpy as jnp
from jax import lax
import numpy as np

D_MODEL = 1024
BATCH = 4
SEQ = 4096
DEPTH = 1
DEC_BATCH = 128
DEC_SEQ = 8
PAST_LEN = 8192
PAGE_SIZE = 128

MIX_DIM = D_MODEL
ATTN_DIM = MIX_DIM // 2
CONV_CH = MIX_DIM - ATTN_DIM
HEAD_DIM = 64
N_HEADS = ATTN_DIM // HEAD_DIM
N_KV_HEADS = 2
GQA_GROUP = N_HEADS // N_KV_HEADS
KV_DIM = N_KV_HEADS * HEAD_DIM
WINDOW = 128
ATTN_BLOCK = WINDOW
CONV_WIDTH = 31
IN_DIM = ATTN_DIM + 2 * KV_DIM + 2 * CONV_CH
SPLITS = [ATTN_DIM, ATTN_DIM + KV_DIM, ATTN_DIM + 2 * KV_DIM, ATTN_DIM + 2 * KV_DIM + CONV_CH]
PEER_HEADS = 8
N_KEYS = 128
N_EXPERTS = N_KEYS * N_KEYS
PEER_TOPK = 16
PEER_QDIM = 256
PEER_HALF = PEER_QDIM // 2
PEER_CHUNK = 128
EPS = 1e-6
NEG = -1e30

kernel_name = "hymba_swa_sink_conformer_peer_step"


def rms_norm(x, g):
    xf = x.astype(jnp.float32)
    y = xf * lax.rsqrt(jnp.mean(xf * xf, axis=-1, keepdims=True) + EPS)
    return (y * g.astype(jnp.float32)).astype(x.dtype)


def layer_norm(x, g, b):
    xf = x.astype(jnp.float32)
    mu = jnp.mean(xf, axis=-1, keepdims=True)
    var = jnp.mean(jnp.square(xf - mu), axis=-1, keepdims=True)
    y = (xf - mu) * lax.rsqrt(var + EPS)
    return (y * g.astype(jnp.float32) + b.astype(jnp.float32)).astype(x.dtype)


def _project(xn, w_in, q_g, k_g):
    lead = xn.shape[:-1]
    z = xn @ w_in
    q, k, v, val, gate = jnp.split(z, SPLITS, axis=-1)
    q = rms_norm(q.reshape(*lead, N_KV_HEADS, GQA_GROUP, HEAD_DIM), q_g)
    k = rms_norm(k.reshape(*lead, N_KV_HEADS, HEAD_DIM), k_g)
    v = v.reshape(*lead, N_KV_HEADS, HEAD_DIM)
    glu = val * jax.nn.sigmoid(gate)
    return q, k, v, glu


def _sink_attend(q, k, v, mask, sinks):
    s = jnp.einsum('...qkgd,...skd->...kgqs', q, k).astype(jnp.float32) * (HEAD_DIM ** -0.5)
    s = jnp.where(mask, s, NEG)
    sink = sinks.astype(jnp.float32).reshape(N_KV_HEADS, GQA_GROUP, 1, 1)
    m = jnp.maximum(jnp.max(s, axis=-1, keepdims=True), sink)
    p = jnp.exp(s - m)
    p = p / (jnp.sum(p, axis=-1, keepdims=True) + jnp.exp(sink - m))
    return jnp.einsum('...kgqs,...skd->...qkgd', p.astype(v.dtype), v)


def _attn_prompt(q, k, v, sinks):
    B, T = q.shape[0], q.shape[1]
    nb = T // ATTN_BLOCK
    qb = q.reshape(B, nb, ATTN_BLOCK, N_KV_HEADS, GQA_GROUP, HEAD_DIM)
    kb = k.reshape(B, nb, ATTN_BLOCK, N_KV_HEADS, HEAD_DIM)
    vb = v.reshape(B, nb, ATTN_BLOCK, N_KV_HEADS, HEAD_DIM)
    prev = lambda t: jnp.concatenate([jnp.zeros_like(t[:, :1]), t[:, :-1]], axis=1)
    kk = jnp.concatenate([prev(kb), kb], axis=2)
    vv = jnp.concatenate([prev(vb), vb], axis=2)
    start = jnp.arange(nb)[:, None, None] * ATTN_BLOCK
    qpos = start + jnp.arange(ATTN_BLOCK)[None, :, None]
    kpos = start + jnp.arange(2 * ATTN_BLOCK)[None, None, :] - ATTN_BLOCK
    mask = (kpos <= qpos) & (kpos > qpos - WINDOW) & (kpos >= 0)
    o = _sink_attend(qb, kk, vv, mask[:, None, None], sinks)
    return o.reshape(B, T, ATTN_DIM)


def _attn_sample(q, k, v, ck, cv, sinks):
    Bd, S = q.shape[0], q.shape[1]
    wb = ck.shape[1]
    kk = jnp.concatenate([ck, k], axis=1)
    vv = jnp.concatenate([cv, v], axis=1)
    qpos = PAST_LEN + jnp.arange(S)[:, None]
    kpos = PAST_LEN - wb + jnp.arange(wb + S)[None, :]
    mask = (kpos <= qpos) & (kpos > qpos - WINDOW)
    o = _sink_attend(q, kk, vv, mask, sinks)
    return o.reshape(Bd, S, ATTN_DIM), kk[:, -wb:], vv[:, -wb:]


def _conv_module(u_padded, w, b, ln_g, ln_b, pw):
    c = lax.conv_general_dilated(u_padded, w[:, None, :], window_strides=(1,), padding='VALID',
                                 dimension_numbers=('NWC', 'WIO', 'NWC'),
                                 feature_group_count=CONV_CH) + b
    return jax.nn.silu(layer_norm(c, ln_g, ln_b)) @ pw


def _merge(a, c, ga, gc, w_out):
    return jnp.concatenate([rms_norm(a, ga), rms_norm(c, gc)], axis=-1) @ w_out


def _peer(xn, wq, keys, u, v):
    lead = xn.shape[:-1]
    xt = xn.reshape(-1, D_MODEL)
    T = xt.shape[0]
    q = (xt @ wq).reshape(T, PEER_HEADS, 2, PEER_HALF).astype(jnp.float32)
    s = jnp.einsum('thpd,hpnd->thpn', q, keys.astype(jnp.float32))
    ts, ti = lax.top_k(s, PEER_TOPK)
    cand_s = (ts[:, :, 0, :, None] + ts[:, :, 1, None, :]).reshape(T, PEER_HEADS, PEER_TOPK * PEER_TOPK)
    cand_i = (ti[:, :, 0, :, None] * N_KEYS + ti[:, :, 1, None, :]).reshape(T, PEER_HEADS, PEER_TOPK * PEER_TOPK)
    best_s, pos = lax.top_k(cand_s, PEER_TOPK)
    idx = jnp.take_along_axis(cand_i, pos, axis=-1)
    gate = jax.nn.softmax(best_s, axis=-1).astype(xn.dtype)
    n_chunks = -(-T // PEER_CHUNK)
    pad = n_chunks * PEER_CHUNK - T
    xc = jnp.pad(xt, ((0, pad), (0, 0))).reshape(n_chunks, PEER_CHUNK, D_MODEL)
    ic = jnp.pad(idx, ((0, pad), (0, 0), (0, 0))).reshape(n_chunks, PEER_CHUNK, PEER_HEADS, PEER_TOPK)
    gc = jnp.pad(gate, ((0, pad), (0, 0), (0, 0))).reshape(n_chunks, PEER_CHUNK, PEER_HEADS, PEER_TOPK)

    def block(args):
        xb, ib, gb = args
        hu = jnp.einsum('chkd,cd->chk', jnp.take(u, ib, axis=0), xb)
        act = jax.nn.gelu(hu, approximate=False) * gb
        return jnp.einsum('chk,chkd->cd', act, jnp.take(v, ib, axis=0))

    out = lax.map(block, (xc, ic, gc)).reshape(-1, D_MODEL)[:T]
    return out.reshape(*lead, D_MODEL)


def setup_inputs(seed: int = 0) -> dict:
    key = jax.random.key(seed)
    ks = jax.random.split(key, 24)
    f32 = jnp.float32
    wbuf = min(WINDOW, PAST_LEN)
    nrm = lambda k, shape, scale: jax.random.normal(k, shape, f32) * scale
    gain = lambda k, shape: 1.0 + 0.02 * jax.random.normal(k, shape, f32)
    return {
        'x_prompt': nrm(ks[0], (BATCH, SEQ, D_MODEL), 1.0),
        'x_sample': nrm(ks[1], (DEC_BATCH, DEC_SEQ, D_MODEL), 1.0),
        'cache_k': nrm(ks[2], (DEPTH, DEC_BATCH, wbuf, N_KV_HEADS, HEAD_DIM), 1.0),
        'cache_v': nrm(ks[3], (DEPTH, DEC_BATCH, wbuf, N_KV_HEADS, HEAD_DIM), 1.0),
        'state_conv': nrm(ks[4], (DEPTH, DEC_BATCH, CONV_WIDTH - 1, CONV_CH), 0.5),
        'norm1_g': gain(ks[5], (DEPTH, D_MODEL)),
        'w_in': nrm(ks[6], (DEPTH, D_MODEL, IN_DIM), D_MODEL ** -0.5),
        'q_norm_g': gain(ks[7], (DEPTH, HEAD_DIM)),
        'k_norm_g': gain(ks[8], (DEPTH, HEAD_DIM)),
        'attn_sinks': nrm(ks[9], (DEPTH, N_HEADS), 0.5),
        'conv_w': nrm(ks[10], (DEPTH, CONV_WIDTH, CONV_CH), CONV_WIDTH ** -0.5),
        'conv_b': nrm(ks[11], (DEPTH, CONV_CH), 0.02),
        'conv_ln_g': gain(ks[12], (DEPTH, CONV_CH)),
        'conv_ln_b': nrm(ks[13], (DEPTH, CONV_CH), 0.02),
        'conv_pw_w': nrm(ks[14], (DEPTH, CONV_CH, CONV_CH), CONV_CH ** -0.5),
        'attn_out_g': gain(ks[15], (DEPTH, ATTN_DIM)),
        'conv_out_g': gain(ks[16], (DEPTH, CONV_CH)),
        'w_out': nrm(ks[17], (DEPTH, MIX_DIM, D_MODEL), MIX_DIM ** -0.5),
        'norm2_g': gain(ks[18], (DEPTH, D_MODEL)),
        'peer_wq': nrm(ks[19], (DEPTH, D_MODEL, PEER_HEADS * PEER_QDIM), D_MODEL ** -0.5),
        'peer_keys': nrm(ks[20], (DEPTH, PEER_HEADS, 2, N_KEYS, PEER_HALF), PEER_HALF ** -0.5),
        'peer_u': nrm(ks[21], (DEPTH, N_EXPERTS, D_MODEL), D_MODEL ** -0.5),
        'peer_v': nrm(ks[22], (DEPTH, N_EXPERTS, D_MODEL), 0.5),
    }


def reference(x_prompt, x_sample, cache_k, cache_v, state_conv, norm1_g, w_in, q_norm_g, k_norm_g,
              attn_sinks, conv_w, conv_b, conv_ln_g, conv_ln_b, conv_pw_w, attn_out_g, conv_out_g,
              w_out, norm2_g, peer_wq, peer_keys, peer_u, peer_v):
    xp, xs = x_prompt, x_sample
    pk, pv, pc, sk, sv, sc = [], [], [], [], [], []
    for l in range(DEPTH):
        xn = rms_norm(xp, norm1_g[l])
        q, k, v, glu = _project(xn, w_in[l], q_norm_g[l], k_norm_g[l])
        a = _attn_prompt(q, k, v, attn_sinks[l])
        zpad = jnp.zeros((xp.shape[0], CONV_WIDTH - 1, CONV_CH), glu.dtype)
        c = _conv_module(jnp.concatenate([zpad, glu], axis=1), conv_w[l], conv_b[l],
                         conv_ln_g[l], conv_ln_b[l], conv_pw_w[l])
        h = xp + _merge(a, c, attn_out_g[l], conv_out_g[l], w_out[l])
        xp = h + _peer(rms_norm(h, norm2_g[l]), peer_wq[l], peer_keys[l], peer_u[l], peer_v[l])
        pk.append(k[:, -WINDOW:])
        pv.append(v[:, -WINDOW:])
        pc.append(glu[:, -(CONV_WIDTH - 1):])
        xn = rms_norm(xs, norm1_g[l])
        q, k, v, glu = _project(xn, w_in[l], q_norm_g[l], k_norm_g[l])
        a, nk, nv = _attn_sample(q, k, v, cache_k[l], cache_v[l], attn_sinks[l])
        u_full = jnp.concatenate([state_conv[l], glu], axis=1)
        c = _conv_module(u_full, conv_w[l], conv_b[l], conv_ln_g[l], conv_ln_b[l], conv_pw_w[l])
        h = xs + _merge(a, c, attn_out_g[l], conv_out_g[l], w_out[l])
        xs = h + _peer(rms_norm(h, norm2_g[l]), peer_wq[l], peer_keys[l], peer_u[l], peer_v[l])
        sk.append(nk)
        sv.append(nv)
        sc.append(u_full[:, -(CONV_WIDTH - 1):])
    return (xp, xs, jnp.stack(pk), jnp.stack(pv), jnp.stack(pc), jnp.stack(sk), jnp.stack(sv), jnp.stack(sc))
```

```python
import functools

import numpy as np
import jax
import jax.numpy as jnp
from jax import lax
from jax.experimental import pallas as pl
from jax.experimental.pallas import tpu as pltpu

F32 = jnp.float32
BF16 = jnp.bfloat16

D_MODEL = 1024
ATTN_DIM = 512
CONV_CH = 512
HEAD_DIM = 64
N_KV_HEADS = 2
GQA_GROUP = 4
KV_DIM = N_KV_HEADS * HEAD_DIM
WINDOW = 128
CONV_WIDTH = 31
IN_DIM = ATTN_DIM + 2 * KV_DIM + 2 * CONV_CH
QK_DIM = ATTN_DIM + KV_DIM
PEER_HEADS = 8
N_KEYS = 128
N_EXPERTS = N_KEYS * N_KEYS
PEER_TOPK = 16
PEER_HALF = 128
EPS = 1e-6
NEG = -1e30

SUBLANES = 8
LANES = 128
HIST_ROWS = 32
HIST_SKIP = HIST_ROWS - (CONV_WIDTH - 1)

VMEM_LIMIT = 56 * 1024 * 1024


def _params(*sem):
    return pltpu.CompilerParams(dimension_semantics=sem, vmem_limit_bytes=VMEM_LIMIT)


def _proj_kernel(x_ref, g1_ref, w_ref, seg_ref, gqk_ref, q_ref, k_ref, v_ref, glu_ref):
    x = x_ref[...]
    xn = x * lax.rsqrt(jnp.mean(x * x, axis=-1, keepdims=True) + EPS) * g1_ref[...]
    z = jnp.dot(xn.astype(BF16), w_ref[...], preferred_element_type=F32)
    qk = z[:, :QK_DIM]
    sq = qk * qk
    hi = sq.astype(BF16)
    lo = (sq - hi.astype(F32)).astype(BF16)
    seg = seg_ref[...]
    ssum = jnp.dot(hi, seg, preferred_element_type=F32) + jnp.dot(lo, seg, preferred_element_type=F32)
    qkn = qk * lax.rsqrt(ssum * (1.0 / HEAD_DIM) + EPS) * gqk_ref[...]
    q_ref[...] = qkn[:, :ATTN_DIM]
    k_ref[...] = qkn[:, ATTN_DIM:]
    v_ref[...] = z[:, QK_DIM:QK_DIM + KV_DIM]
    val = z[:, QK_DIM + KV_DIM:QK_DIM + KV_DIM + CONV_CH]
    gate = z[:, QK_DIM + KV_DIM + CONV_CH:]
    glu_ref[...] = val * jax.nn.sigmoid(gate)


def _proj(x, g1, w_in_bf, seg, gqk, tm):
    t = x.shape[0]
    row = lambda n: pl.BlockSpec((tm, n), lambda i: (i, 0))
    full = lambda a: pl.BlockSpec(a.shape, lambda i: (0,) * a.ndim)
    return pl.pallas_call(
        _proj_kernel,
        grid=(t // tm,),
        in_specs=[row(D_MODEL), full(g1), full(w_in_bf), full(seg), full(gqk)],
        out_specs=[row(ATTN_DIM), row(KV_DIM), row(KV_DIM), row(CONV_CH)],
        out_shape=[jax.ShapeDtypeStruct((t, n), F32) for n in (ATTN_DIM, KV_DIM, KV_DIM, CONV_CH)],
        compiler_params=_params("parallel"),
        name="proj",
    )(x, g1, w_in_bf, seg, gqk)


def _attend(q, kp, kc, vp, vc, sink_ref, first):
    s_rows = q.shape[0]
    n_keys = WINDOW + s_rows
    kk = jnp.concatenate([kp, kc], axis=0).astype(BF16)
    vv = jnp.concatenate([vp, vc], axis=0).astype(BF16)
    qs = jnp.concatenate([q[:, c * LANES:(c + 1) * LANES] for c in range(GQA_GROUP)], axis=0)
    lane = lax.broadcasted_iota(jnp.int32, (1, LANES), 1)
    low = lane < HEAD_DIM
    qi = lax.broadcasted_iota(jnp.int32, (s_rows, n_keys), 0)
    kr = lax.broadcasted_iota(jnp.int32, (s_rows, n_keys), 1)
    vis = (kr > qi) & (kr <= qi + WINDOW)
    if first is not None:
        vis = vis & ((kr >= WINDOW) | jnp.logical_not(first))
    vis = jnp.concatenate([vis] * GQA_GROUP, axis=0)
    outs = []
    for j in range(N_KV_HEADS):
        qm = jnp.where(low if j == 0 else jnp.logical_not(low), qs, 0.0).astype(BF16)
        s = lax.dot_general(qm, kk, (((1,), (1,)), ((), ())), preferred_element_type=F32)
        s = jnp.where(vis, s, NEG)
        sink = sink_ref[j]
        m = jnp.maximum(jnp.max(s, axis=-1, keepdims=True), sink)
        p = jnp.exp(s - m)
        denom = jnp.sum(p, axis=-1, keepdims=True) + jnp.exp(sink - m)
        o = jnp.dot(p.astype(BF16), vv, preferred_element_type=F32)
        outs.append(o / denom)
    o = jnp.where(low, outs[0], outs[1])
    return [o[c * s_rows:(c + 1) * s_rows] for c in range(GQA_GROUP)]


def _attn_prompt_kernel(q_ref, kp_ref, kc_ref, vp_ref, vc_ref, sink_ref, a_ref):
    first = pl.program_id(1) == 0
    cols = _attend(q_ref[...], kp_ref[...], kc_ref[...], vp_ref[...], vc_ref[...], sink_ref, first)
    for c in range(GQA_GROUP):
        a_ref[:, c * LANES:(c + 1) * LANES] = cols[c]


def _attn_prompt(q, k, v, sink_rows):
    b, t, _ = q.shape
    nb = t // WINDOW
    cur = lambda n: pl.BlockSpec((None, WINDOW, n), lambda i, j: (i, j, 0))
    prev = lambda n: pl.BlockSpec((None, WINDOW, n), lambda i, j: (i, jnp.maximum(j - 1, 0), 0))
    return pl.pallas_call(
        _attn_prompt_kernel,
        grid=(b, nb),
        in_specs=[cur(ATTN_DIM), prev(KV_DIM), cur(KV_DIM), prev(KV_DIM), cur(KV_DIM),
                  pl.BlockSpec(sink_rows.shape, lambda i, j: (0, 0, 0))],
        out_specs=cur(ATTN_DIM),
        out_shape=jax.ShapeDtypeStruct((b, t, ATTN_DIM), F32),
        compiler_params=_params("parallel", "parallel"),
        name="attn_prompt",
    )(q, k, k, v, v, sink_rows)


def _attn_sample_kernel(q_ref, kp_ref, kc_ref, vp_ref, vc_ref, sink_ref, a_ref, *, seqs):
    def one(i, carry):
        cols = _attend(q_ref[i], kp_ref[i], kc_ref[i], vp_ref[i], vc_ref[i], sink_ref, None)
        a_ref[i] = jnp.concatenate(cols, axis=1)
        return carry
    lax.fori_loop(0, seqs, one, 0)


def _attn_sample(q, ck, k, cv, v, sink_rows, seqs):
    b, s, _ = q.shape
    blk = lambda r, n: pl.BlockSpec((seqs, r, n), lambda i: (i, 0, 0))
    return pl.pallas_call(
        functools.partial(_attn_sample_kernel, seqs=seqs),
        grid=(b // seqs,),
        in_specs=[blk(s, ATTN_DIM), blk(WINDOW, KV_DIM), blk(s, KV_DIM), blk(WINDOW, KV_DIM), blk(s, KV_DIM),
                  pl.BlockSpec(sink_rows.shape, lambda i: (0, 0, 0))],
        out_specs=blk(s, ATTN_DIM),
        out_shape=jax.ShapeDtypeStruct((b, s, ATTN_DIM), F32),
        compiler_params=_params("parallel"),
        name="attn_sample",
    )(q, ck, k, cv, v, sink_rows)


def _conv_kernel(hist_ref, cur_ref, w_ref, b_ref, lg_ref, lb_ref, pw_ref, out_ref, buf_ref, acc_ref,
                 *, seqs, rows, chunk):
    def one(i, carry):
        buf_ref[0:HIST_ROWS, :] = hist_ref[i]
        buf_ref[HIST_ROWS:HIST_ROWS + rows, :] = cur_ref[i]
        base = pl.multiple_of(i * rows, SUBLANES)
        for c in range(rows // chunk):
            acc = jnp.broadcast_to(b_ref[...], (chunk, CONV_CH))
            for j in range(CONV_WIDTH):
                lo = c * chunk + HIST_SKIP + j
                acc = acc + buf_ref[lo:lo + chunk, :] * w_ref[j:j + 1, :]
            acc_ref[pl.ds(base + c * chunk, chunk), :] = acc
        return carry
    lax.fori_loop(0, seqs, one, 0)
    c = acc_ref[...]
    mu = jnp.mean(c, axis=-1, keepdims=True)
    d = c - mu
    var = jnp.mean(d * d, axis=-1, keepdims=True)
    y = d * lax.rsqrt(var + EPS) * lg_ref[...] + lb_ref[...]
    y = y * jax.nn.sigmoid(y)
    out_ref[...] = jnp.dot(y.astype(BF16), pw_ref[...], preferred_element_type=F32)


def _conv(hist, cur, w, b, lg, lb, pw_bf, seqs, chunk):
    n, rows, _ = cur.shape
    full = lambda a: pl.BlockSpec(a.shape, lambda i: (0,) * a.ndim)
    return pl.pallas_call(
        functools.partial(_conv_kernel, seqs=seqs, rows=rows, chunk=chunk),
        grid=(n // seqs,),
        in_specs=[pl.BlockSpec((seqs, HIST_ROWS, CONV_CH), lambda i: (i, 0, 0)),
                  pl.BlockSpec((seqs, rows, CONV_CH), lambda i: (i, 0, 0)),
                  full(w), full(b), full(lg), full(lb), full(pw_bf)],
        out_specs=pl.BlockSpec((seqs * rows, CONV_CH), lambda i: (i, 0)),
        out_shape=jax.ShapeDtypeStruct((n * rows, CONV_CH), F32),
        scratch_shapes=[pltpu.VMEM((HIST_ROWS + rows, CONV_CH), F32),
                        pltpu.VMEM((seqs * rows, CONV_CH), F32)],
        compiler_params=_params("parallel"),
        name="conv",
    )(hist, cur, w, b, lg, lb, pw_bf)


def _rms(x, g):
    return x * lax.rsqrt(jnp.mean(x * x, axis=-1, keepdims=True) + EPS) * g


def _merge_kernel(x_ref, a_ref, c_ref, ga_ref, gc_ref, wa_ref, wc_ref, g2_ref, h_ref, xn_ref):
    an = _rms(a_ref[...], ga_ref[...]).astype(BF16)
    cn = _rms(c_ref[...], gc_ref[...]).astype(BF16)
    h = (x_ref[...] + jnp.dot(an, wa_ref[...], preferred_element_type=F32)
         + jnp.dot(cn, wc_ref[...], preferred_element_type=F32))
    h_ref[...] = h
    xn_ref[...] = _rms(h, g2_ref[...]).astype(BF16)


def _merge(x, a, c, ga, gc, wa_bf, wc_bf, g2, tm):
    t = x.shape[0]
    row = lambda n: pl.BlockSpec((tm, n), lambda i: (i, 0))
    full = lambda a_: pl.BlockSpec(a_.shape, lambda i: (0,) * a_.ndim)
    return pl.pallas_call(
        _merge_kernel,
        grid=(t // tm,),
        in_specs=[row(D_MODEL), row(ATTN_DIM), row(CONV_CH), full(ga), full(gc), full(wa_bf), full(wc_bf), full(g2)],
        out_specs=[row(D_MODEL), row(D_MODEL)],
        out_shape=[jax.ShapeDtypeStruct((t, D_MODEL), F32), jax.ShapeDtypeStruct((t, D_MODEL), BF16)],
        compiler_params=_params("parallel"),
        name="merge",
    )(x, a, c, ga, gc, wa_bf, wc_bf, g2)


def _oddeven_merge_sort_pairs(n):
    pairs = []
    p = 1
    while p < n:
        k = p
        while k >= 1:
            for j in range(k % p, n - k, 2 * k):
                for i in range(min(k, n - j - k)):
                    if (i + j) // (2 * p) == (i + j + k) // (2 * p):
                        pairs.append((i + j, i + j + k))
            k //= 2
        p *= 2
    return pairs


_SORT16 = _oddeven_merge_sort_pairs(PEER_TOPK)


def _sort_desc(v):
    v = list(v)
    for i, j in _SORT16:
        v[i], v[j] = jnp.maximum(v[i], v[j]), jnp.minimum(v[i], v[j])
    return v


def _bitonic_to_desc(z):
    z = list(z)
    d = PEER_TOPK // 2
    while d >= 1:
        for i in range(PEER_TOPK):
            if i & d == 0:
                z[i], z[i + d] = jnp.maximum(z[i], z[i + d]), jnp.minimum(z[i], z[i + d])
        d //= 2
    return z


def _merge_top(x, y):
    return _bitonic_to_desc([jnp.maximum(x[i], y[PEER_TOPK - 1 - i]) for i in range(PEER_TOPK)])


def _across_sublanes(x):
    for shift in (4, 2, 1):
        x = _merge_top(x, [pltpu.roll(t, shift, 0) for t in x])
    return x


def _top16(tile):
    v = [tile[i * SUBLANES:(i + 1) * SUBLANES, :] for i in range(N_KEYS // SUBLANES)]
    return _across_sublanes(_sort_desc(v))


def _best_sums(a, b):
    sub = lax.broadcasted_iota(jnp.int32, (SUBLANES, LANES), 0)
    a_lo, a_hi = a[0], a[SUBLANES]
    for r in range(1, SUBLANES):
        a_lo = jnp.where(sub == r, a[r], a_lo)
        a_hi = jnp.where(sub == r, a[SUBLANES + r], a_hi)
    x = [jnp.where(sub < PEER_TOPK // (j + 1), a_lo + b[j], -jnp.inf) for j in range(PEER_TOPK)]
    x[PEER_TOPK - 1] = jnp.maximum(x[PEER_TOPK - 1], a_hi + b[0])
    return _across_sublanes(_bitonic_to_desc(x))


def _scores_kernel(xn_ref, wq_ref, keys_ref, s1_ref, s2_ref, e1_ref, e2_ref, thr_ref, s_ref, *, tb):
    xn = xn_ref[...]
    for hp in range(2 * PEER_HEADS):
        qt = lax.dot_general(wq_ref[hp * PEER_HALF:(hp + 1) * PEER_HALF, :], xn, (((1,), (1,)), ((), ())),
                             preferred_element_type=F32)
        s_ref[hp] = jnp.dot(keys_ref[hp], qt.astype(BF16), preferred_element_type=F32)

    groups = tb // LANES

    def one(it, carry):
        h = it // groups
        lanes = pl.ds(pl.multiple_of((it % groups) * LANES, LANES), LANES)
        s1 = s_ref[2 * h, :, lanes]
        s2 = s_ref[2 * h + 1, :, lanes]
        a = _top16(s1)
        b = _top16(s2)
        c = _best_sums(a, b)
        z = jnp.ones_like(c[0])
        for k in range(1, PEER_TOPK):
            z = z + jnp.exp(c[k] - c[0])
        rep = lambda t: jnp.concatenate([t] * (N_KEYS // SUBLANES), axis=0)
        s1_ref[h, :, lanes] = s1
        s2_ref[h, :, lanes] = s2
        thr_ref[h, :, lanes] = c[PEER_TOPK - 1]
        e1_ref[h, :, lanes] = jnp.exp(s1 - rep(a[0])) * rep(1.0 / z)
        e2_ref[h, :, lanes] = jnp.exp(s2 - rep(b[0]))
        return carry

    lax.fori_loop(0, PEER_HEADS * groups, one, 0)


def _scores(xn, wq_t_bf, keys_bf, tb):
    t = xn.shape[0]
    out = pl.BlockSpec((PEER_HEADS, N_KEYS, tb), lambda i: (0, 0, i))
    return pl.pallas_call(
        functools.partial(_scores_kernel, tb=tb),
        grid=(t // tb,),
        in_specs=[pl.BlockSpec((tb, D_MODEL), lambda i: (i, 0)),
                  pl.BlockSpec(wq_t_bf.shape, lambda i: (0, 0)),
                  pl.BlockSpec(keys_bf.shape, lambda i: (0, 0, 0))],
        out_specs=[out] * 4 + [pl.BlockSpec((PEER_HEADS, SUBLANES, tb), lambda i: (0, 0, i))],
        out_shape=[jax.ShapeDtypeStruct((PEER_HEADS, N_KEYS, t), F32)] * 4
                  + [jax.ShapeDtypeStruct((PEER_HEADS, SUBLANES, t), F32)],
        scratch_shapes=[pltpu.VMEM((2 * PEER_HEADS, N_KEYS, tb), F32)],
        compiler_params=_params("parallel"),
        name="peer_scores",
    )(xn, wq_t_bf, keys_bf)


ROWS_BF16 = 2 * SUBLANES


def _experts_kernel(xn_ref, h_ref, u_ref, vt_ref, s1_ref, s2_ref, e1_ref, e2_ref, thr_ref, y_ref,
                    hu_ref, w_ref, acc_ref, *, tb, eb):
    e = pl.program_id(1)

    @pl.when(e == 0)
    def _():
        acc_ref[...] = jnp.zeros_like(acc_ref)

    hu_ref[...] = lax.dot_general(u_ref[...], xn_ref[...], (((1,), (1,)), ((), ())),
                                  preferred_element_type=F32)
    def one(grp, carry):
        lanes = pl.ds(pl.multiple_of(grp * LANES, LANES), LANES)
        thr = [jnp.broadcast_to(thr_ref[h, 0:1, lanes], (ROWS_BF16, LANES)) for h in range(PEER_HEADS)]
        for i1 in range(eb // N_KEYS):
            s1 = [jnp.broadcast_to(s1_ref[h, i1:i1 + 1, lanes], (ROWS_BF16, LANES)) for h in range(PEER_HEADS)]
            e1 = [jnp.broadcast_to(e1_ref[h, i1:i1 + 1, lanes], (ROWS_BF16, LANES)) for h in range(PEER_HEADS)]
            for r in range(N_KEYS // ROWS_BF16):
                rows = slice(r * ROWS_BF16, (r + 1) * ROWS_BF16)
                g = jnp.zeros((ROWS_BF16, LANES), F32)
                for h in range(PEER_HEADS):
                    picked = s1[h] + s2_ref[h, rows, lanes] >= thr[h]
                    g = g + jnp.where(picked, e1[h] * e2_ref[h, rows, lanes], 0.0)
                erow = slice(i1 * N_KEYS + r * ROWS_BF16, i1 * N_KEYS + (r + 1) * ROWS_BF16)
                hu = hu_ref[erow, lanes]
                act = 0.5 * hu * (1.0 + lax.erf(hu * np.float32(2.0 ** -0.5)))
                w_ref[erow, lanes] = (act * g).astype(BF16)
        return carry

    lax.fori_loop(0, tb // LANES, one, 0)
    acc_ref[...] += jnp.dot(vt_ref[...], w_ref[...], preferred_element_type=F32)

    @pl.when(e == pl.num_programs(1) - 1)
    def _():
        y_ref[...] = h_ref[...] + acc_ref[...].T


def _experts(xn, h, u_bf, vt_bf, s1, s2, e1, e2, thr, tb, eb):
    t = xn.shape[0]
    tok = lambda n: pl.BlockSpec((tb, n), lambda i, j: (i, 0))
    assert eb // N_KEYS == SUBLANES
    first = pl.BlockSpec((PEER_HEADS, SUBLANES, tb), lambda i, j: (0, j, i))
    second = pl.BlockSpec((PEER_HEADS, N_KEYS, tb), lambda i, j: (0, 0, i))
    return pl.pallas_call(
        functools.partial(_experts_kernel, tb=tb, eb=eb),
        grid=(t // tb, N_EXPERTS // eb),
        in_specs=[tok(D_MODEL), tok(D_MODEL),
                  pl.BlockSpec((eb, D_MODEL), lambda i, j: (j, 0)),
                  pl.BlockSpec((D_MODEL, eb), lambda i, j: (0, j)),
                  first, second, first, second, pl.BlockSpec((PEER_HEADS, SUBLANES, tb), lambda i, j: (0, 0, i))],
        out_specs=tok(D_MODEL),
        out_shape=jax.ShapeDtypeStruct((t, D_MODEL), F32),
        scratch_shapes=[pltpu.VMEM((eb, tb), F32), pltpu.VMEM((eb, tb), BF16), pltpu.VMEM((D_MODEL, tb), F32)],
        compiler_params=_params("parallel", "arbitrary"),
        name="peer_experts",
    )(xn, h, u_bf, vt_bf, s1, s2, e1, e2, thr)


def _q_perm():
    n = np.arange(ATTN_DIM)
    g, j, d = n // LANES, (n % LANES) // HEAD_DIM, n % HEAD_DIM
    return j * (GQA_GROUP * HEAD_DIM) + g * HEAD_DIM + d


def _segment_matrix():
    i = np.arange(QK_DIM)
    return (i[:, None] // HEAD_DIM == i[None, :] // HEAD_DIM).astype(np.float32)


def _sink_rows(sinks, s_rows):
    s = sinks.astype(F32).reshape(N_KV_HEADS, GQA_GROUP, 1, 1)
    return jnp.broadcast_to(s, (N_KV_HEADS, GQA_GROUP, s_rows, 1)).reshape(N_KV_HEADS, GQA_GROUP * s_rows, 1)


def _layer(l, xp, xs, cache_k, cache_v, state_conv, p):
    (norm1_g, w_in, q_norm_g, k_norm_g, attn_sinks, conv_w, conv_b, conv_ln_g, conv_ln_b, conv_pw_w,
     attn_out_g, conv_out_g, w_out, norm2_g, peer_wq, peer_keys, peer_u, peer_v) = [a[l] for a in p]
    bsz, seq, _ = xp.shape
    dbs, dseq, _ = xs.shape
    perm = _q_perm()

    w_in_bf = jnp.concatenate([w_in[:, :ATTN_DIM][:, perm], w_in[:, ATTN_DIM:]], axis=1).astype(BF16)
    seg = jnp.asarray(_segment_matrix(), BF16)
    gqk = jnp.concatenate([jnp.tile(q_norm_g, N_KV_HEADS * GQA_GROUP) * (HEAD_DIM ** -0.5),
                           jnp.tile(k_norm_g, N_KV_HEADS)])[None, :]
    g1 = norm1_g[None, :]
    ga = attn_out_g[perm][None, :]
    gc = conv_out_g[None, :]
    wa_bf = w_out[:ATTN_DIM][perm].astype(BF16)
    wc_bf = w_out[ATTN_DIM:].astype(BF16)
    g2 = norm2_g[None, :]
    cw = jnp.pad(conv_w, ((0, 1), (0, 0)))
    cb, lg, lb = conv_b[None, :], conv_ln_g[None, :], conv_ln_b[None, :]
    pw_bf = conv_pw_w.astype(BF16)
    wq_t_bf = peer_wq.T.astype(BF16)
    keys_bf = peer_keys.reshape(2 * PEER_HEADS, N_KEYS, PEER_HALF).astype(BF16)
    u_bf = peer_u.astype(BF16)
    vt_bf = peer_v.T.astype(BF16)

    def peer(xn, h, tb):
        s1, s2, e1, e2, thr = _scores(xn, wq_t_bf, keys_bf, tb)
        return _experts(xn, h, u_bf, vt_bf, s1, s2, e1, e2, thr, tb, SUBLANES * N_KEYS)

    tp = bsz * seq
    xpf = xp.reshape(tp, D_MODEL)
    q, k, v, glu = _proj(xpf, g1, w_in_bf, seg, gqk, 512)
    q3, k3, v3 = q.reshape(bsz, seq, ATTN_DIM), k.reshape(bsz, seq, KV_DIM), v.reshape(bsz, seq, KV_DIM)
    a = _attn_prompt(q3, k3, v3, _sink_rows(attn_sinks, WINDOW)).reshape(tp, ATTN_DIM)
    tile = 512
    glu4 = glu.reshape(bsz, seq // tile, tile, CONV_CH)
    hist = jnp.concatenate([jnp.zeros((bsz, 1, HIST_ROWS, CONV_CH), F32), glu4[:, :-1, -HIST_ROWS:]], axis=1)
    c = _conv(hist.reshape(-1, HIST_ROWS, CONV_CH), glu4.reshape(-1, tile, CONV_CH), cw, cb, lg, lb, pw_bf, 1, 32)
    h, xn = _merge(xpf, a, c, ga, gc, wa_bf, wc_bf, g2, 512)
    yp = peer(xn, h, 512).reshape(bsz, seq, D_MODEL)
    pk = k3[:, -WINDOW:].reshape(bsz, WINDOW, N_KV_HEADS, HEAD_DIM)
    pv = v3[:, -WINDOW:].reshape(bsz, WINDOW, N_KV_HEADS, HEAD_DIM)
    pc = glu.reshape(bsz, seq, CONV_CH)[:, -(CONV_WIDTH - 1):]

    ts = dbs * dseq
    xsf = xs.reshape(ts, D_MODEL)
    q, k, v, glu = _proj(xsf, g1, w_in_bf, seg, gqk, 512)
    q3, k3, v3 = q.reshape(dbs, dseq, ATTN_DIM), k.reshape(dbs, dseq, KV_DIM), v.reshape(dbs, dseq, KV_DIM)
    ck = cache_k.reshape(dbs, -1, KV_DIM)
    cv = cache_v.reshape(dbs, -1, KV_DIM)
    a = _attn_sample(q3, ck, k3, cv, v3, _sink_rows(attn_sinks, dseq), 32).reshape(ts, ATTN_DIM)
    glu3 = glu.reshape(dbs, dseq, CONV_CH)
    hist = jnp.pad(state_conv, ((0, 0), (HIST_SKIP, 0), (0, 0)))
    c = _conv(hist, glu3, cw, cb, lg, lb, pw_bf, 64, dseq)
    h, xn = _merge(xsf, a, c, ga, gc, wa_bf, wc_bf, g2, 512)
    ys = peer(xn, h, 512).reshape(dbs, dseq, D_MODEL)
    wb = ck.shape[1]
    sk = jnp.concatenate([ck, k3], axis=1)[:, -wb:].reshape(dbs, wb, N_KV_HEADS, HEAD_DIM)
    sv = jnp.concatenate([cv, v3], axis=1)[:, -wb:].reshape(dbs, wb, N_KV_HEADS, HEAD_DIM)
    sc = jnp.concatenate([state_conv, glu3], axis=1)[:, -(CONV_WIDTH - 1):]
    return yp, ys, (pk, pv, pc, sk, sv, sc)


def kernel(x_prompt, x_sample, cache_k, cache_v, state_conv, norm1_g, w_in, q_norm_g, k_norm_g, attn_sinks,
           conv_w, conv_b, conv_ln_g, conv_ln_b, conv_pw_w, attn_out_g, conv_out_g, w_out, norm2_g, peer_wq,
           peer_keys, peer_u, peer_v):
    params = (norm1_g, w_in, q_norm_g, k_norm_g, attn_sinks, conv_w, conv_b, conv_ln_g, conv_ln_b, conv_pw_w,
              attn_out_g, conv_out_g, w_out, norm2_g, peer_wq, peer_keys, peer_u, peer_v)
    depth = w_in.shape[0]
    xp, xs = x_prompt, x_sample
    states = []
    for l in range(depth):
        xp, xs, st = _layer(l, xp, xs, cache_k[l], cache_v[l], state_conv[l], params)
        states.append(st)
    stacked = [jnp.stack([st[i] for st in states]) for i in range(6)]
    return (xp, xs, *stacked)
```

```python
import functools

import numpy as np
import jax
import jax.numpy as jnp
from jax import lax
from jax.experimental import pallas as pl
from jax.experimental.pallas import tpu as pltpu

F32 = jnp.float32
BF16 = jnp.bfloat16

D_MODEL = 1024
ATTN_DIM = 512
CONV_CH = 512
HEAD_DIM = 64
N_KV_HEADS = 2
GQA_GROUP = 4
KV_DIM = N_KV_HEADS * HEAD_DIM
WINDOW = 128
CONV_WIDTH = 31
IN_DIM = ATTN_DIM + 2 * KV_DIM + 2 * CONV_CH
QK_DIM = ATTN_DIM + KV_DIM
PEER_HEADS = 8
N_KEYS = 128
N_EXPERTS = N_KEYS * N_KEYS
PEER_TOPK = 16
PEER_HALF = 128
EPS = 1e-6
NEG = -1e30

SUBLANES = 8
LANES = 128
HIST_ROWS = 32
HIST_SKIP = HIST_ROWS - (CONV_WIDTH - 1)

VMEM_LIMIT = 56 * 1024 * 1024


def _params(*sem):
    return pltpu.CompilerParams(dimension_semantics=sem, vmem_limit_bytes=VMEM_LIMIT)


def _proj_kernel(x_ref, g1_ref, w_ref, seg_ref, gqk_ref, q_ref, k_ref, v_ref, glu_ref):
    x = x_ref[...]
    xn = x * lax.rsqrt(jnp.mean(x * x, axis=-1, keepdims=True) + EPS) * g1_ref[...]
    z = jnp.dot(xn.astype(BF16), w_ref[...], preferred_element_type=F32)
    qk = z[:, :QK_DIM]
    sq = qk * qk
    hi = sq.astype(BF16)
    lo = (sq - hi.astype(F32)).astype(BF16)
    seg = seg_ref[...]
    ssum = jnp.dot(hi, seg, preferred_element_type=F32) + jnp.dot(lo, seg, preferred_element_type=F32)
    qkn = qk * lax.rsqrt(ssum * (1.0 / HEAD_DIM) + EPS) * gqk_ref[...]
    q_ref[...] = qkn[:, :ATTN_DIM]
    k_ref[...] = qkn[:, ATTN_DIM:]
    v_ref[...] = z[:, QK_DIM:QK_DIM + KV_DIM]
    val = z[:, QK_DIM + KV_DIM:QK_DIM + KV_DIM + CONV_CH]
    gate = z[:, QK_DIM + KV_DIM + CONV_CH:]
    glu_ref[...] = val * jax.nn.sigmoid(gate)


def _proj(x, g1, w_in_bf, seg, gqk, tm):
    t = x.shape[0]
    row = lambda n: pl.BlockSpec((tm, n), lambda i: (i, 0))
    full = lambda a: pl.BlockSpec(a.shape, lambda i: (0,) * a.ndim)
    return pl.pallas_call(
        _proj_kernel,
        grid=(t // tm,),
        in_specs=[row(D_MODEL), full(g1), full(w_in_bf), full(seg), full(gqk)],
        out_specs=[row(ATTN_DIM), row(KV_DIM), row(KV_DIM), row(CONV_CH)],
        out_shape=[jax.ShapeDtypeStruct((t, n), F32) for n in (ATTN_DIM, KV_DIM, KV_DIM, CONV_CH)],
        compiler_params=_params("parallel"),
        name="proj",
    )(x, g1, w_in_bf, seg, gqk)


def _attend(q, kp, kc, vp, vc, sink_ref, first):
    s_rows = q.shape[0]
    n_keys = WINDOW + s_rows
    kk = jnp.concatenate([kp, kc], axis=0).astype(BF16)
    vv = jnp.concatenate([vp, vc], axis=0).astype(BF16)
    qs = jnp.concatenate([q[:, c * LANES:(c + 1) * LANES] for c in range(GQA_GROUP)], axis=0)
    lane = lax.broadcasted_iota(jnp.int32, (1, LANES), 1)
    low = lane < HEAD_DIM
    qi = lax.broadcasted_iota(jnp.int32, (s_rows, n_keys), 0)
    kr = lax.broadcasted_iota(jnp.int32, (s_rows, n_keys), 1)
    vis = (kr > qi) & (kr <= qi + WINDOW)
    if first is not None:
        vis = vis & ((kr >= WINDOW) | jnp.logical_not(first))
    vis = jnp.concatenate([vis] * GQA_GROUP, axis=0)
    outs = []
    for j in range(N_KV_HEADS):
        qm = jnp.where(low if j == 0 else jnp.logical_not(low), qs, 0.0).astype(BF16)
        s = lax.dot_general(qm, kk, (((1,), (1,)), ((), ())), preferred_element_type=F32)
        s = jnp.where(vis, s, NEG)
        sink = sink_ref[j]
        m = jnp.maximum(jnp.max(s, axis=-1, keepdims=True), sink)
        p = jnp.exp(s - m)
        denom = jnp.sum(p, axis=-1, keepdims=True) + jnp.exp(sink - m)
        o = jnp.dot(p.astype(BF16), vv, preferred_element_type=F32)
        outs.append(o / denom)
    o = jnp.where(low, outs[0], outs[1])
    return [o[c * s_rows:(c + 1) * s_rows] for c in range(GQA_GROUP)]


def _attn_prompt_kernel(q_ref, kp_ref, kc_ref, vp_ref, vc_ref, sink_ref, a_ref):
    first = pl.program_id(1) == 0
    cols = _attend(q_ref[...], kp_ref[...], kc_ref[...], vp_ref[...], vc_ref[...], sink_ref, first)
    for c in range(GQA_GROUP):
        a_ref[:, c * LANES:(c + 1) * LANES] = cols[c]


def _attn_prompt(q, k, v, sink_rows):
    b, t, _ = q.shape
    nb = t // WINDOW
    cur = lambda n: pl.BlockSpec((None, WINDOW, n), lambda i, j: (i, j, 0))
    prev = lambda n: pl.BlockSpec((None, WINDOW, n), lambda i, j: (i, jnp.maximum(j - 1, 0), 0))
    return pl.pallas_call(
        _attn_prompt_kernel,
        grid=(b, nb),
        in_specs=[cur(ATTN_DIM), prev(KV_DIM), cur(KV_DIM), prev(KV_DIM), cur(KV_DIM),
                  pl.BlockSpec(sink_rows.shape, lambda i, j: (0, 0, 0))],
        out_specs=cur(ATTN_DIM),
        out_shape=jax.ShapeDtypeStruct((b, t, ATTN_DIM), F32),
        compiler_params=_params("parallel", "parallel"),
        name="attn_prompt",
    )(q, k, k, v, v, sink_rows)


def _attn_sample_kernel(q_ref, kp_ref, kc_ref, vp_ref, vc_ref, sink_ref, a_ref, *, seqs):
    def one(i, carry):
        cols = _attend(q_ref[i], kp_ref[i], kc_ref[i], vp_ref[i], vc_ref[i], sink_ref, None)
        a_ref[i] = jnp.concatenate(cols, axis=1)
        return carry
    lax.fori_loop(0, seqs, one, 0)


def _attn_sample(q, ck, k, cv, v, sink_rows, seqs):
    b, s, _ = q.shape
    blk = lambda r, n: pl.BlockSpec((seqs, r, n), lambda i: (i, 0, 0))
    return pl.pallas_call(
        functools.partial(_attn_sample_kernel, seqs=seqs),
        grid=(b // seqs,),
        in_specs=[blk(s, ATTN_DIM), blk(WINDOW, KV_DIM), blk(s, KV_DIM), blk(WINDOW, KV_DIM), blk(s, KV_DIM),
                  pl.BlockSpec(sink_rows.shape, lambda i: (0, 0, 0))],
        out_specs=blk(s, ATTN_DIM),
        out_shape=jax.ShapeDtypeStruct((b, s, ATTN_DIM), F32),
        compiler_params=_params("parallel"),
        name="attn_sample",
    )(q, ck, k, cv, v, sink_rows)


def _conv_kernel(hist_ref, cur_ref, w_ref, b_ref, lg_ref, lb_ref, pw_ref, out_ref, buf_ref, acc_ref,
                 *, seqs, rows, chunk):
    def one(i, carry):
        buf_ref[0:HIST_ROWS, :] = hist_ref[i]
        buf_ref[HIST_ROWS:HIST_ROWS + rows, :] = cur_ref[i]
        base = pl.multiple_of(i * rows, SUBLANES)
        for c in range(rows // chunk):
            acc = jnp.broadcast_to(b_ref[...], (chunk, CONV_CH))
            for j in range(CONV_WIDTH):
                lo = c * chunk + HIST_SKIP + j
                acc = acc + buf_ref[lo:lo + chunk, :] * w_ref[j:j + 1, :]
            acc_ref[pl.ds(base + c * chunk, chunk), :] = acc
        return carry
    lax.fori_loop(0, seqs, one, 0)
    c = acc_ref[...]
    mu = jnp.mean(c, axis=-1, keepdims=True)
    d = c - mu
    var = jnp.mean(d * d, axis=-1, keepdims=True)
    y = d * lax.rsqrt(var + EPS) * lg_ref[...] + lb_ref[...]
    y = y * jax.nn.sigmoid(y)
    out_ref[...] = jnp.dot(y.astype(BF16), pw_ref[...], preferred_element_type=F32)


def _conv(hist, cur, w, b, lg, lb, pw_bf, seqs, chunk):
    n, rows, _ = cur.shape
    full = lambda a: pl.BlockSpec(a.shape, lambda i: (0,) * a.ndim)
    return pl.pallas_call(
        functools.partial(_conv_kernel, seqs=seqs, rows=rows, chunk=chunk),
        grid=(n // seqs,),
        in_specs=[pl.BlockSpec((seqs, HIST_ROWS, CONV_CH), lambda i: (i, 0, 0)),
                  pl.BlockSpec((seqs, rows, CONV_CH), lambda i: (i, 0, 0)),
                  full(w), full(b), full(lg), full(lb), full(pw_bf)],
        out_specs=pl.BlockSpec((seqs * rows, CONV_CH), lambda i: (i, 0)),
        out_shape=jax.ShapeDtypeStruct((n * rows, CONV_CH), F32),
        scratch_shapes=[pltpu.VMEM((HIST_ROWS + rows, CONV_CH), F32),
                        pltpu.VMEM((seqs * rows, CONV_CH), F32)],
        compiler_params=_params("parallel"),
        name="conv",
    )(hist, cur, w, b, lg, lb, pw_bf)


def _rms(x, g):
    return x * lax.rsqrt(jnp.mean(x * x, axis=-1, keepdims=True) + EPS) * g


def _merge_kernel(x_ref, a_ref, c_ref, ga_ref, gc_ref, wa_ref, wc_ref, g2_ref, h_ref, xn_ref):
    an = _rms(a_ref[...], ga_ref[...]).astype(BF16)
    cn = _rms(c_ref[...], gc_ref[...]).astype(BF16)
    h = (x_ref[...] + jnp.dot(an, wa_ref[...], preferred_element_type=F32)
         + jnp.dot(cn, wc_ref[...], preferred_element_type=F32))
    h_ref[...] = h
    xn_ref[...] = _rms(h, g2_ref[...]).astype(BF16)


def _merge(x, a, c, ga, gc, wa_bf, wc_bf, g2, tm):
    t = x.shape[0]
    row = lambda n: pl.BlockSpec((tm, n), lambda i: (i, 0))
    full = lambda a_: pl.BlockSpec(a_.shape, lambda i: (0,) * a_.ndim)
    return pl.pallas_call(
        _merge_kernel,
        grid=(t // tm,),
        in_specs=[row(D_MODEL), row(ATTN_DIM), row(CONV_CH), full(ga), full(gc), full(wa_bf), full(wc_bf), full(g2)],
        out_specs=[row(D_MODEL), row(D_MODEL)],
        out_shape=[jax.ShapeDtypeStruct((t, D_MODEL), F32), jax.ShapeDtypeStruct((t, D_MODEL), BF16)],
        compiler_params=_params("parallel"),
        name="merge",
    )(x, a, c, ga, gc, wa_bf, wc_bf, g2)


def _oddeven_merge_sort_pairs(n):
    pairs = []
    p = 1
    while p < n:
        k = p
        while k >= 1:
            for j in range(k % p, n - k, 2 * k):
                for i in range(min(k, n - j - k)):
                    if (i + j) // (2 * p) == (i + j + k) // (2 * p):
                        pairs.append((i + j, i + j + k))
            k //= 2
        p *= 2
    return pairs


_SORT16 = _oddeven_merge_sort_pairs(PEER_TOPK)


def _sort_desc(v):
    v = list(v)
    for i, j in _SORT16:
        v[i], v[j] = jnp.maximum(v[i], v[j]), jnp.minimum(v[i], v[j])
    return v


def _bitonic_to_desc(z):
    z = list(z)
    d = PEER_TOPK // 2
    while d >= 1:
        for i in range(PEER_TOPK):
            if i & d == 0:
                z[i], z[i + d] = jnp.maximum(z[i], z[i + d]), jnp.minimum(z[i], z[i + d])
        d //= 2
    return z


def _merge_top(x, y):
    return _bitonic_to_desc([jnp.maximum(x[i], y[PEER_TOPK - 1 - i]) for i in range(PEER_TOPK)])


def _across_sublanes(x):
    for shift in (4, 2, 1):
        x = _merge_top(x, [pltpu.roll(t, shift, 0) for t in x])
    return x


def _top16(tile):
    v = [tile[i * SUBLANES:(i + 1) * SUBLANES, :] for i in range(N_KEYS // SUBLANES)]
    return _across_sublanes(_sort_desc(v))


def _best_sums(a, b):
    sub = lax.broadcasted_iota(jnp.int32, (SUBLANES, LANES), 0)
    a_lo, a_hi = a[0], a[SUBLANES]
    for r in range(1, SUBLANES):
        a_lo = jnp.where(sub == r, a[r], a_lo)
        a_hi = jnp.where(sub == r, a[SUBLANES + r], a_hi)
    x = [jnp.where(sub < PEER_TOPK // (j + 1), a_lo + b[j], -jnp.inf) for j in range(PEER_TOPK)]
    x[PEER_TOPK - 1] = jnp.maximum(x[PEER_TOPK - 1], a_hi + b[0])
    return _across_sublanes(_bitonic_to_desc(x))


def _first_key_cuts(a, b, thr):
    cuts = []
    for j in range(PEER_TOPK):
        cut = jnp.full_like(thr, jnp.inf)
        for k in range(PEER_TOPK // (j + 1)):
            cut = jnp.where(a[k] + b[j] >= thr, a[k], cut)
        cuts.append(cut)
    return cuts


ROWS_BF16 = 2 * SUBLANES
ROW_BLOCKS = N_KEYS // ROWS_BF16
PAIR_ROWS = 2 * ROWS_BF16
SECOND_ROWS = ROW_BLOCKS * PEER_HEADS * PAIR_ROWS


def _scores_kernel(xn_ref, wq_ref, keys_ref, cnt_ref, e1_ref, sec_ref, s_ref, *, tb):
    xn = xn_ref[...]
    for hp in range(2 * PEER_HEADS):
        qt = lax.dot_general(wq_ref[hp * PEER_HALF:(hp + 1) * PEER_HALF, :], xn, (((1,), (1,)), ((), ())),
                             preferred_element_type=F32)
        s_ref[hp] = jnp.dot(keys_ref[hp], qt.astype(BF16), preferred_element_type=F32)

    groups = tb // LANES
    tiles = N_KEYS // SUBLANES

    def one(it, carry):
        h = it // groups
        grp = it % groups
        lanes = pl.ds(pl.multiple_of(grp * LANES, LANES), LANES)
        s1 = s_ref[2 * h, :, lanes]
        s2 = s_ref[2 * h + 1, :, lanes]
        a = _top16(s1)
        b = _top16(s2)
        c = _best_sums(a, b)
        z = jnp.ones_like(c[0])
        for k in range(1, PEER_TOPK):
            z = z + jnp.exp(c[k] - c[0])
        inv_z = 1.0 / z
        cuts = _first_key_cuts(a, b, c[PEER_TOPK - 1])
        cnt, e1, rank, e2 = [], [], [], []
        for i in range(tiles):
            x1 = s1[i * SUBLANES:(i + 1) * SUBLANES, :]
            x2 = s2[i * SUBLANES:(i + 1) * SUBLANES, :]
            n = jnp.zeros_like(x1)
            r = jnp.zeros_like(x2)
            for k in range(PEER_TOPK):
                n = jnp.where(x1 >= cuts[k], k + 1.0, n)
                r = jnp.where(b[k] > x2, k + 1.0, r)
            cnt.append(n)
            rank.append(r)
            e1.append(jnp.exp(x1 - a[0]) * inv_z)
            e2.append(jnp.exp(x2 - b[0]))
        cnt_ref[h, grp] = jnp.concatenate(cnt, axis=0)
        e1_ref[h, grp] = jnp.concatenate(e1, axis=0)
        for r in range(ROW_BLOCKS):
            row = pl.multiple_of((r * PEER_HEADS + h) * PAIR_ROWS, PAIR_ROWS)
            pair = jnp.concatenate(rank[2 * r:2 * r + 2] + e2[2 * r:2 * r + 2], axis=0)
            sec_ref[grp, pl.ds(row, PAIR_ROWS), :] = pair.astype(BF16)
        return carry

    lax.fori_loop(0, PEER_HEADS * groups, one, 0)


def _scores(xn, wq_t_bf, keys_bf, tb):
    t = xn.shape[0]
    groups = tb // LANES
    first = pl.BlockSpec((PEER_HEADS, groups, N_KEYS, LANES), lambda i: (0, i, 0, 0))
    return pl.pallas_call(
        functools.partial(_scores_kernel, tb=tb),
        grid=(t // tb,),
        in_specs=[pl.BlockSpec((tb, D_MODEL), lambda i: (i, 0)),
                  pl.BlockSpec(wq_t_bf.shape, lambda i: (0, 0)),
                  pl.BlockSpec(keys_bf.shape, lambda i: (0, 0, 0))],
        out_specs=[first, first, pl.BlockSpec((groups, SECOND_ROWS, LANES), lambda i: (i, 0, 0))],
        out_shape=[jax.ShapeDtypeStruct((PEER_HEADS, t // LANES, N_KEYS, LANES), F32)] * 2
                  + [jax.ShapeDtypeStruct((t // LANES, SECOND_ROWS, LANES), BF16)],
        scratch_shapes=[pltpu.VMEM((2 * PEER_HEADS, N_KEYS, tb), F32)],
        compiler_params=_params("parallel"),
        name="peer_scores",
    )(xn, wq_t_bf, keys_bf)


def _experts_kernel(xn_ref, h_ref, u_ref, vt_ref, cnt_ref, e1_ref, sec_ref, y_ref, hu_ref, w_ref, acc_ref,
                    row_ref, *, tb, eb):
    e = pl.program_id(1)
    groups = tb // LANES

    @pl.when(e == 0)
    def _():
        acc_ref[...] = jnp.zeros_like(acc_ref)

    hu = lax.dot_general(u_ref[...], xn_ref[...], (((1,), (1,)), ((), ())), preferred_element_type=F32)
    for g in range(groups):
        hu_ref[g] = hu[:, g * LANES:(g + 1) * LANES]

    def one(grp, carry):
        for i1 in range(eb // N_KEYS):
            for h in range(PEER_HEADS):
                for w, ref in enumerate((cnt_ref, e1_ref)):
                    tile = jnp.broadcast_to(ref[h, grp, i1:i1 + 1, :], (ROWS_BF16, LANES))
                    row_ref[(i1 * PEER_HEADS + h) * 2 + w] = tile.astype(BF16)
        for i1 in range(eb // N_KEYS):
            cnt = [row_ref[(i1 * PEER_HEADS + h) * 2] for h in range(PEER_HEADS)]
            e1 = [row_ref[(i1 * PEER_HEADS + h) * 2 + 1] for h in range(PEER_HEADS)]
            for r in range(ROW_BLOCKS):
                gate = jnp.zeros((ROWS_BF16, LANES), BF16)
                for h in range(PEER_HEADS):
                    base = (r * PEER_HEADS + h) * PAIR_ROWS
                    rank = sec_ref[grp, base:base + ROWS_BF16, :]
                    e2 = sec_ref[grp, base + ROWS_BF16:base + PAIR_ROWS, :]
                    gate = gate + jnp.minimum(e1[h] * e2, jnp.maximum(cnt[h] - rank, 0.0))
                erow = slice(i1 * N_KEYS + r * ROWS_BF16, i1 * N_KEYS + (r + 1) * ROWS_BF16)
                x = hu_ref[grp, erow, :]
                act = 0.5 * x * (1.0 + lax.erf(x * np.float32(2.0 ** -0.5)))
                w_ref[grp, erow, :] = act.astype(BF16) * gate
        return carry

    lax.fori_loop(0, groups, one, 0)
    w = jnp.concatenate([w_ref[g] for g in range(groups)], axis=1)
    acc_ref[...] += jnp.dot(vt_ref[...], w, preferred_element_type=F32)

    @pl.when(e == pl.num_programs(1) - 1)
    def _():
        y_ref[...] = h_ref[...] + acc_ref[...].T


def _experts(xn, h, u_bf, vt_bf, cnt, e1, sec, tb, eb):
    t = xn.shape[0]
    groups = tb // LANES
    tok = lambda n: pl.BlockSpec((tb, n), lambda i, j: (i, 0))
    assert eb // N_KEYS == SUBLANES
    first = pl.BlockSpec((PEER_HEADS, groups, SUBLANES, LANES), lambda i, j: (0, i, j, 0))
    return pl.pallas_call(
        functools.partial(_experts_kernel, tb=tb, eb=eb),
        grid=(t // tb, N_EXPERTS // eb),
        in_specs=[tok(D_MODEL), tok(D_MODEL),
                  pl.BlockSpec((eb, D_MODEL), lambda i, j: (j, 0)),
                  pl.BlockSpec((D_MODEL, eb), lambda i, j: (0, j)),
                  first, first, pl.BlockSpec((groups, SECOND_ROWS, LANES), lambda i, j: (i, 0, 0))],
        out_specs=tok(D_MODEL),
        out_shape=jax.ShapeDtypeStruct((t, D_MODEL), F32),
        scratch_shapes=[pltpu.VMEM((groups, eb, LANES), F32), pltpu.VMEM((groups, eb, LANES), BF16),
                        pltpu.VMEM((D_MODEL, tb), F32),
                        pltpu.VMEM((2 * PEER_HEADS * (eb // N_KEYS), ROWS_BF16, LANES), BF16)],
        compiler_params=_params("parallel", "arbitrary"),
        name="peer_experts",
    )(xn, h, u_bf, vt_bf, cnt, e1, sec)


def _q_perm():
    n = np.arange(ATTN_DIM)
    g, j, d = n // LANES, (n % LANES) // HEAD_DIM, n % HEAD_DIM
    return j * (GQA_GROUP * HEAD_DIM) + g * HEAD_DIM + d


def _segment_matrix():
    i = np.arange(QK_DIM)
    return (i[:, None] // HEAD_DIM == i[None, :] // HEAD_DIM).astype(np.float32)


def _sink_rows(sinks, s_rows):
    s = sinks.astype(F32).reshape(N_KV_HEADS, GQA_GROUP, 1, 1)
    return jnp.broadcast_to(s, (N_KV_HEADS, GQA_GROUP, s_rows, 1)).reshape(N_KV_HEADS, GQA_GROUP * s_rows, 1)


def _layer(l, xp, xs, cache_k, cache_v, state_conv, p):
    (norm1_g, w_in, q_norm_g, k_norm_g, attn_sinks, conv_w, conv_b, conv_ln_g, conv_ln_b, conv_pw_w,
     attn_out_g, conv_out_g, w_out, norm2_g, peer_wq, peer_keys, peer_u, peer_v) = [a[l] for a in p]
    bsz, seq, _ = xp.shape
    dbs, dseq, _ = xs.shape
    perm = _q_perm()

    w_in_bf = jnp.concatenate([w_in[:, :ATTN_DIM][:, perm], w_in[:, ATTN_DIM:]], axis=1).astype(BF16)
    seg = jnp.asarray(_segment_matrix(), BF16)
    gqk = jnp.concatenate([jnp.tile(q_norm_g, N_KV_HEADS * GQA_GROUP) * (HEAD_DIM ** -0.5),
                           jnp.tile(k_norm_g, N_KV_HEADS)])[None, :]
    g1 = norm1_g[None, :]
    ga = attn_out_g[perm][None, :]
    gc = conv_out_g[None, :]
    wa_bf = w_out[:ATTN_DIM][perm].astype(BF16)
    wc_bf = w_out[ATTN_DIM:].astype(BF16)
    g2 = norm2_g[None, :]
    cw = jnp.pad(conv_w, ((0, 1), (0, 0)))
    cb, lg, lb = conv_b[None, :], conv_ln_g[None, :], conv_ln_b[None, :]
    pw_bf = conv_pw_w.astype(BF16)
    wq_t_bf = peer_wq.T.astype(BF16)
    keys_bf = peer_keys.reshape(2 * PEER_HEADS, N_KEYS, PEER_HALF).astype(BF16)
    u_bf = peer_u.astype(BF16)
    vt_bf = peer_v.T.astype(BF16)

    def peer(xn, h, tb):
        cnt, e1, sec = _scores(xn, wq_t_bf, keys_bf, tb)
        return _experts(xn, h, u_bf, vt_bf, cnt, e1, sec, tb, SUBLANES * N_KEYS)

    tp = bsz * seq
    xpf = xp.reshape(tp, D_MODEL)
    q, k, v, glu = _proj(xpf, g1, w_in_bf, seg, gqk, 512)
    q3, k3, v3 = q.reshape(bsz, seq, ATTN_DIM), k.reshape(bsz, seq, KV_DIM), v.reshape(bsz, seq, KV_DIM)
    a = _attn_prompt(q3, k3, v3, _sink_rows(attn_sinks, WINDOW)).reshape(tp, ATTN_DIM)
    tile = 512
    glu4 = glu.reshape(bsz, seq // tile, tile, CONV_CH)
    hist = jnp.concatenate([jnp.zeros((bsz, 1, HIST_ROWS, CONV_CH), F32), glu4[:, :-1, -HIST_ROWS:]], axis=1)
    c = _conv(hist.reshape(-1, HIST_ROWS, CONV_CH), glu4.reshape(-1, tile, CONV_CH), cw, cb, lg, lb, pw_bf, 1, 32)
    h, xn = _merge(xpf, a, c, ga, gc, wa_bf, wc_bf, g2, 512)
    yp = peer(xn, h, 512).reshape(bsz, seq, D_MODEL)
    pk = k3[:, -WINDOW:].reshape(bsz, WINDOW, N_KV_HEADS, HEAD_DIM)
    pv = v3[:, -WINDOW:].reshape(bsz, WINDOW, N_KV_HEADS, HEAD_DIM)
    pc = glu.reshape(bsz, seq, CONV_CH)[:, -(CONV_WIDTH - 1):]

    ts = dbs * dseq
    xsf = xs.reshape(ts, D_MODEL)
    q, k, v, glu = _proj(xsf, g1, w_in_bf, seg, gqk, 512)
    q3, k3, v3 = q.reshape(dbs, dseq, ATTN_DIM), k.reshape(dbs, dseq, KV_DIM), v.reshape(dbs, dseq, KV_DIM)
    ck = cache_k.reshape(dbs, -1, KV_DIM)
    cv = cache_v.reshape(dbs, -1, KV_DIM)
    a = _attn_sample(q3, ck, k3, cv, v3, _sink_rows(attn_sinks, dseq), 32).reshape(ts, ATTN_DIM)
    glu3 = glu.reshape(dbs, dseq, CONV_CH)
    hist = jnp.pad(state_conv, ((0, 0), (HIST_SKIP, 0), (0, 0)))
    c = _conv(hist, glu3, cw, cb, lg, lb, pw_bf, 64, dseq)
    h, xn = _merge(xsf, a, c, ga, gc, wa_bf, wc_bf, g2, 512)
    ys = peer(xn, h, 512).reshape(dbs, dseq, D_MODEL)
    wb = ck.shape[1]
    sk = jnp.concatenate([ck, k3], axis=1)[:, -wb:].reshape(dbs, wb, N_KV_HEADS, HEAD_DIM)
    sv = jnp.concatenate([cv, v3], axis=1)[:, -wb:].reshape(dbs, wb, N_KV_HEADS, HEAD_DIM)
    sc = jnp.concatenate([state_conv, glu3], axis=1)[:, -(CONV_WIDTH - 1):]
    return yp, ys, (pk, pv, pc, sk, sv, sc)


def kernel(x_prompt, x_sample, cache_k, cache_v, state_conv, norm1_g, w_in, q_norm_g, k_norm_g, attn_sinks,
           conv_w, conv_b, conv_ln_g, conv_ln_b, conv_pw_w, attn_out_g, conv_out_g, w_out, norm2_g, peer_wq,
           peer_keys, peer_u, peer_v):
    params = (norm1_g, w_in, q_norm_g, k_norm_g, attn_sinks, conv_w, conv_b, conv_ln_g, conv_ln_b, conv_pw_w,
              attn_out_g, conv_out_g, w_out, norm2_g, peer_wq, peer_keys, peer_u, peer_v)
    depth = w_in.shape[0]
    xp, xs = x_prompt, x_sample
    states = []
    for l in range(depth):
        xp, xs, st = _layer(l, xp, xs, cache_k[l], cache_v[l], state_conv[l], params)
        states.append(st)
    stacked = [jnp.stack([st[i] for st in states]) for i in range(6)]
    return (xp, xs, *stacked)
```

```python
import functools

import numpy as np
import jax
import jax.numpy as jnp
from jax import lax
from jax.experimental import pallas as pl
from jax.experimental.pallas import tpu as pltpu

F32 = jnp.float32
BF16 = jnp.bfloat16

D_MODEL = 1024
ATTN_DIM = 512
CONV_CH = 512
HEAD_DIM = 64
N_KV_HEADS = 2
GQA_GROUP = 4
KV_DIM = N_KV_HEADS * HEAD_DIM
WINDOW = 128
CONV_WIDTH = 31
IN_DIM = ATTN_DIM + 2 * KV_DIM + 2 * CONV_CH
QK_DIM = ATTN_DIM + KV_DIM
PEER_HEADS = 8
N_KEYS = 128
N_EXPERTS = N_KEYS * N_KEYS
PEER_TOPK = 16
PEER_HALF = 128
EPS = 1e-6
NEG = -1e30

SUBLANES = 8
LANES = 128
HIST_ROWS = 32
HIST_SKIP = HIST_ROWS - (CONV_WIDTH - 1)

VMEM_LIMIT = 56 * 1024 * 1024


def _params(*sem):
    return pltpu.CompilerParams(dimension_semantics=sem, vmem_limit_bytes=VMEM_LIMIT)


def _proj_kernel(x_ref, g1_ref, w_ref, seg_ref, gqk_ref, q_ref, k_ref, v_ref, glu_ref):
    x = x_ref[...]
    xn = x * lax.rsqrt(jnp.mean(x * x, axis=-1, keepdims=True) + EPS) * g1_ref[...]
    z = jnp.dot(xn.astype(BF16), w_ref[...], preferred_element_type=F32)
    qk = z[:, :QK_DIM]
    sq = qk * qk
    hi = sq.astype(BF16)
    lo = (sq - hi.astype(F32)).astype(BF16)
    seg = seg_ref[...]
    ssum = jnp.dot(hi, seg, preferred_element_type=F32) + jnp.dot(lo, seg, preferred_element_type=F32)
    qkn = qk * lax.rsqrt(ssum * (1.0 / HEAD_DIM) + EPS) * gqk_ref[...]
    q_ref[...] = qkn[:, :ATTN_DIM]
    k_ref[...] = qkn[:, ATTN_DIM:]
    v_ref[...] = z[:, QK_DIM:QK_DIM + KV_DIM]
    val = z[:, QK_DIM + KV_DIM:QK_DIM + KV_DIM + CONV_CH]
    gate = z[:, QK_DIM + KV_DIM + CONV_CH:]
    glu_ref[...] = val * jax.nn.sigmoid(gate)


def _proj(x, g1, w_in_bf, seg, gqk, tm):
    t = x.shape[0]
    row = lambda n: pl.BlockSpec((tm, n), lambda i: (i, 0))
    full = lambda a: pl.BlockSpec(a.shape, lambda i: (0,) * a.ndim)
    return pl.pallas_call(
        _proj_kernel,
        grid=(t // tm,),
        in_specs=[row(D_MODEL), full(g1), full(w_in_bf), full(seg), full(gqk)],
        out_specs=[row(ATTN_DIM), row(KV_DIM), row(KV_DIM), row(CONV_CH)],
        out_shape=[jax.ShapeDtypeStruct((t, n), F32) for n in (ATTN_DIM, KV_DIM, KV_DIM, CONV_CH)],
        compiler_params=_params("parallel"),
        name="proj",
    )(x, g1, w_in_bf, seg, gqk)


def _attend(q, kp, kc, vp, vc, sink_ref, first):
    s_rows = q.shape[0]
    n_keys = WINDOW + s_rows
    kk = jnp.concatenate([kp, kc], axis=0).astype(BF16)
    vv = jnp.concatenate([vp, vc], axis=0).astype(BF16)
    qs = jnp.concatenate([q[:, c * LANES:(c + 1) * LANES] for c in range(GQA_GROUP)], axis=0)
    lane = lax.broadcasted_iota(jnp.int32, (1, LANES), 1)
    low = lane < HEAD_DIM
    qi = lax.broadcasted_iota(jnp.int32, (s_rows, n_keys), 0)
    kr = lax.broadcasted_iota(jnp.int32, (s_rows, n_keys), 1)
    vis = (kr > qi) & (kr <= qi + WINDOW)
    if first is not None:
        vis = vis & ((kr >= WINDOW) | jnp.logical_not(first))
    vis = jnp.concatenate([vis] * GQA_GROUP, axis=0)
    outs = []
    for j in range(N_KV_HEADS):
        qm = jnp.where(low if j == 0 else jnp.logical_not(low), qs, 0.0).astype(BF16)
        s = lax.dot_general(qm, kk, (((1,), (1,)), ((), ())), preferred_element_type=F32)
        s = jnp.where(vis, s, NEG)
        sink = sink_ref[j]
        m = jnp.maximum(jnp.max(s, axis=-1, keepdims=True), sink)
        p = jnp.exp(s - m)
        denom = jnp.sum(p, axis=-1, keepdims=True) + jnp.exp(sink - m)
        o = jnp.dot(p.astype(BF16), vv, preferred_element_type=F32)
        outs.append(o / denom)
    o = jnp.where(low, outs[0], outs[1])
    return [o[c * s_rows:(c + 1) * s_rows] for c in range(GQA_GROUP)]


def _attn_prompt_kernel(q_ref, kp_ref, kc_ref, vp_ref, vc_ref, sink_ref, a_ref):
    first = pl.program_id(1) == 0
    cols = _attend(q_ref[...], kp_ref[...], kc_ref[...], vp_ref[...], vc_ref[...], sink_ref, first)
    for c in range(GQA_GROUP):
        a_ref[:, c * LANES:(c + 1) * LANES] = cols[c]


def _attn_prompt(q, k, v, sink_rows):
    b, t, _ = q.shape
    nb = t // WINDOW
    cur = lambda n: pl.BlockSpec((None, WINDOW, n), lambda i, j: (i, j, 0))
    prev = lambda n: pl.BlockSpec((None, WINDOW, n), lambda i, j: (i, jnp.maximum(j - 1, 0), 0))
    return pl.pallas_call(
        _attn_prompt_kernel,
        grid=(b, nb),
        in_specs=[cur(ATTN_DIM), prev(KV_DIM), cur(KV_DIM), prev(KV_DIM), cur(KV_DIM),
                  pl.BlockSpec(sink_rows.shape, lambda i, j: (0, 0, 0))],
        out_specs=cur(ATTN_DIM),
        out_shape=jax.ShapeDtypeStruct((b, t, ATTN_DIM), F32),
        compiler_params=_params("parallel", "parallel"),
        name="attn_prompt",
    )(q, k, k, v, v, sink_rows)


def _attn_sample_kernel(q_ref, kp_ref, kc_ref, vp_ref, vc_ref, sink_ref, a_ref, *, seqs):
    def one(i, carry):
        cols = _attend(q_ref[i], kp_ref[i], kc_ref[i], vp_ref[i], vc_ref[i], sink_ref, None)
        a_ref[i] = jnp.concatenate(cols, axis=1)
        return carry
    lax.fori_loop(0, seqs, one, 0)


def _attn_sample(q, ck, k, cv, v, sink_rows, seqs):
    b, s, _ = q.shape
    blk = lambda r, n: pl.BlockSpec((seqs, r, n), lambda i: (i, 0, 0))
    return pl.pallas_call(
        functools.partial(_attn_sample_kernel, seqs=seqs),
        grid=(b // seqs,),
        in_specs=[blk(s, ATTN_DIM), blk(WINDOW, KV_DIM), blk(s, KV_DIM), blk(WINDOW, KV_DIM), blk(s, KV_DIM),
                  pl.BlockSpec(sink_rows.shape, lambda i: (0, 0, 0))],
        out_specs=blk(s, ATTN_DIM),
        out_shape=jax.ShapeDtypeStruct((b, s, ATTN_DIM), F32),
        compiler_params=_params("parallel"),
        name="attn_sample",
    )(q, ck, k, cv, v, sink_rows)


def _conv_kernel(hist_ref, cur_ref, w_ref, b_ref, lg_ref, lb_ref, pw_ref, out_ref, buf_ref, acc_ref,
                 *, seqs, rows, chunk):
    def one(i, carry):
        buf_ref[0:HIST_ROWS, :] = hist_ref[i]
        buf_ref[HIST_ROWS:HIST_ROWS + rows, :] = cur_ref[i]
        base = pl.multiple_of(i * rows, SUBLANES)
        for c in range(rows // chunk):
            acc = jnp.broadcast_to(b_ref[...], (chunk, CONV_CH))
            for j in range(CONV_WIDTH):
                lo = c * chunk + HIST_SKIP + j
                acc = acc + buf_ref[lo:lo + chunk, :] * w_ref[j:j + 1, :]
            acc_ref[pl.ds(base + c * chunk, chunk), :] = acc
        return carry
    lax.fori_loop(0, seqs, one, 0)
    c = acc_ref[...]
    mu = jnp.mean(c, axis=-1, keepdims=True)
    d = c - mu
    var = jnp.mean(d * d, axis=-1, keepdims=True)
    y = d * lax.rsqrt(var + EPS) * lg_ref[...] + lb_ref[...]
    y = y * jax.nn.sigmoid(y)
    out_ref[...] = jnp.dot(y.astype(BF16), pw_ref[...], preferred_element_type=F32)


def _conv(hist, cur, w, b, lg, lb, pw_bf, seqs, chunk):
    n, rows, _ = cur.shape
    full = lambda a: pl.BlockSpec(a.shape, lambda i: (0,) * a.ndim)
    return pl.pallas_call(
        functools.partial(_conv_kernel, seqs=seqs, rows=rows, chunk=chunk),
        grid=(n // seqs,),
        in_specs=[pl.BlockSpec((seqs, HIST_ROWS, CONV_CH), lambda i: (i, 0, 0)),
                  pl.BlockSpec((seqs, rows, CONV_CH), lambda i: (i, 0, 0)),
                  full(w), full(b), full(lg), full(lb), full(pw_bf)],
        out_specs=pl.BlockSpec((seqs * rows, CONV_CH), lambda i: (i, 0)),
        out_shape=jax.ShapeDtypeStruct((n * rows, CONV_CH), F32),
        scratch_shapes=[pltpu.VMEM((HIST_ROWS + rows, CONV_CH), F32),
                        pltpu.VMEM((seqs * rows, CONV_CH), F32)],
        compiler_params=_params("parallel"),
        name="conv",
    )(hist, cur, w, b, lg, lb, pw_bf)


def _rms(x, g):
    return x * lax.rsqrt(jnp.mean(x * x, axis=-1, keepdims=True) + EPS) * g


def _merge_kernel(x_ref, a_ref, c_ref, ga_ref, gc_ref, wa_ref, wc_ref, g2_ref, h_ref, xn_ref):
    an = _rms(a_ref[...], ga_ref[...]).astype(BF16)
    cn = _rms(c_ref[...], gc_ref[...]).astype(BF16)
    h = (x_ref[...] + jnp.dot(an, wa_ref[...], preferred_element_type=F32)
         + jnp.dot(cn, wc_ref[...], preferred_element_type=F32))
    h_ref[...] = h
    xn_ref[...] = _rms(h, g2_ref[...]).astype(BF16)


def _merge(x, a, c, ga, gc, wa_bf, wc_bf, g2, tm):
    t = x.shape[0]
    row = lambda n: pl.BlockSpec((tm, n), lambda i: (i, 0))
    full = lambda a_: pl.BlockSpec(a_.shape, lambda i: (0,) * a_.ndim)
    return pl.pallas_call(
        _merge_kernel,
        grid=(t // tm,),
        in_specs=[row(D_MODEL), row(ATTN_DIM), row(CONV_CH), full(ga), full(gc), full(wa_bf), full(wc_bf), full(g2)],
        out_specs=[row(D_MODEL), row(D_MODEL)],
        out_shape=[jax.ShapeDtypeStruct((t, D_MODEL), F32), jax.ShapeDtypeStruct((t, D_MODEL), BF16)],
        compiler_params=_params("parallel"),
        name="merge",
    )(x, a, c, ga, gc, wa_bf, wc_bf, g2)


def _oddeven_merge_sort_pairs(n):
    pairs = []
    p = 1
    while p < n:
        k = p
        while k >= 1:
            for j in range(k % p, n - k, 2 * k):
                for i in range(min(k, n - j - k)):
                    if (i + j) // (2 * p) == (i + j + k) // (2 * p):
                        pairs.append((i + j, i + j + k))
            k //= 2
        p *= 2
    return pairs


_SORT16 = _oddeven_merge_sort_pairs(PEER_TOPK)


def _sort_desc(v):
    v = list(v)
    for i, j in _SORT16:
        v[i], v[j] = jnp.maximum(v[i], v[j]), jnp.minimum(v[i], v[j])
    return v


def _bitonic_to_desc(z):
    z = list(z)
    d = PEER_TOPK // 2
    while d >= 1:
        for i in range(PEER_TOPK):
            if i & d == 0:
                z[i], z[i + d] = jnp.maximum(z[i], z[i + d]), jnp.minimum(z[i], z[i + d])
        d //= 2
    return z


def _merge_top(x, y):
    return _bitonic_to_desc([jnp.maximum(x[i], y[PEER_TOPK - 1 - i]) for i in range(PEER_TOPK)])


def _across_sublanes(x):
    for shift in (4, 2, 1):
        x = _merge_top(x, [pltpu.roll(t, shift, 0) for t in x])
    return x


def _top16(tile):
    v = [tile[i * SUBLANES:(i + 1) * SUBLANES, :] for i in range(N_KEYS // SUBLANES)]
    return _across_sublanes(_sort_desc(v))


def _best_sums(a, b):
    sub = lax.broadcasted_iota(jnp.int32, (SUBLANES, LANES), 0)
    a_lo, a_hi = a[0], a[SUBLANES]
    for r in range(1, SUBLANES):
        a_lo = jnp.where(sub == r, a[r], a_lo)
        a_hi = jnp.where(sub == r, a[SUBLANES + r], a_hi)
    x = [jnp.where(sub < PEER_TOPK // (j + 1), a_lo + b[j], -jnp.inf) for j in range(PEER_TOPK)]
    x[PEER_TOPK - 1] = jnp.maximum(x[PEER_TOPK - 1], a_hi + b[0])
    return _across_sublanes(_bitonic_to_desc(x))


def _first_key_cuts(a, b, thr):
    cuts = []
    for j in range(PEER_TOPK):
        cut = jnp.full_like(thr, jnp.inf)
        for k in range(PEER_TOPK // (j + 1)):
            cut = jnp.where(a[k] + b[j] >= thr, a[k], cut)
        cuts.append(cut)
    return cuts


ROWS_BF16 = 2 * SUBLANES
ROW_BLOCKS = N_KEYS // ROWS_BF16
PAIR_ROWS = 2 * ROWS_BF16
SECOND_ROWS = ROW_BLOCKS * PEER_HEADS * PAIR_ROWS


def _scores_kernel(xn_ref, wq_ref, keys_ref, cnt_ref, e1_ref, sec_ref, s_ref, *, tb):
    xn = xn_ref[...]
    for hp in range(2 * PEER_HEADS):
        qt = lax.dot_general(wq_ref[hp * PEER_HALF:(hp + 1) * PEER_HALF, :], xn, (((1,), (1,)), ((), ())),
                             preferred_element_type=F32)
        s_ref[hp] = jnp.dot(keys_ref[hp], qt.astype(BF16), preferred_element_type=F32)

    groups = tb // LANES
    tiles = N_KEYS // SUBLANES

    def one(it, carry):
        h = it // groups
        grp = it % groups
        lanes = pl.ds(pl.multiple_of(grp * LANES, LANES), LANES)
        s1 = s_ref[2 * h, :, lanes]
        s2 = s_ref[2 * h + 1, :, lanes]
        a = _top16(s1)
        b = _top16(s2)
        c = _best_sums(a, b)
        z = jnp.ones_like(c[0])
        for k in range(1, PEER_TOPK):
            z = z + jnp.exp(c[k] - c[0])
        inv_z = 1.0 / z
        cuts = _first_key_cuts(a, b, c[PEER_TOPK - 1])
        cnt, e1, rank, e2 = [], [], [], []
        for i in range(tiles):
            x1 = s1[i * SUBLANES:(i + 1) * SUBLANES, :]
            x2 = s2[i * SUBLANES:(i + 1) * SUBLANES, :]
            n = jnp.zeros_like(x1)
            r = jnp.zeros_like(x2)
            for k in range(PEER_TOPK):
                n = jnp.where(x1 >= cuts[k], k + 1.0, n)
                r = jnp.where(b[k] > x2, k + 1.0, r)
            cnt.append(n)
            rank.append(r)
            e1.append(jnp.exp(x1 - a[0]) * inv_z)
            e2.append(jnp.exp(x2 - b[0]))
        cnt_ref[h, grp] = jnp.concatenate(cnt, axis=0)
        e1_ref[h, grp] = jnp.concatenate(e1, axis=0)
        for r in range(ROW_BLOCKS):
            row = pl.multiple_of((r * PEER_HEADS + h) * PAIR_ROWS, PAIR_ROWS)
            pair = jnp.concatenate(rank[2 * r:2 * r + 2] + e2[2 * r:2 * r + 2], axis=0)
            sec_ref[grp, pl.ds(row, PAIR_ROWS), :] = pair
        return carry

    lax.fori_loop(0, PEER_HEADS * groups, one, 0)


def _scores(xn, wq_t_bf, keys_bf, tb):
    t = xn.shape[0]
    groups = tb // LANES
    first = pl.BlockSpec((PEER_HEADS, groups, N_KEYS, LANES), lambda i: (0, i, 0, 0))
    return pl.pallas_call(
        functools.partial(_scores_kernel, tb=tb),
        grid=(t // tb,),
        in_specs=[pl.BlockSpec((tb, D_MODEL), lambda i: (i, 0)),
                  pl.BlockSpec(wq_t_bf.shape, lambda i: (0, 0)),
                  pl.BlockSpec(keys_bf.shape, lambda i: (0, 0, 0))],
        out_specs=[first, first, pl.BlockSpec((groups, SECOND_ROWS, LANES), lambda i: (i, 0, 0))],
        out_shape=[jax.ShapeDtypeStruct((PEER_HEADS, t // LANES, N_KEYS, LANES), F32)] * 2
                  + [jax.ShapeDtypeStruct((t // LANES, SECOND_ROWS, LANES), F32)],
        scratch_shapes=[pltpu.VMEM((2 * PEER_HEADS, N_KEYS, tb), F32)],
        compiler_params=_params("parallel"),
        name="peer_scores",
    )(xn, wq_t_bf, keys_bf)


CHUNK_TOKENS = 256


def _experts_kernel(xn_ref, h_ref, u_ref, vt_ref, cnt_ref, e1_ref, sec_ref, y_ref, hu_ref, w_ref, acc_ref,
                    *, tb, eb):
    e = pl.program_id(1)
    chunks = tb // CHUNK_TOKENS
    per_chunk = CHUNK_TOKENS // LANES

    @pl.when(e == 0)
    def _():
        acc_ref[...] = jnp.zeros_like(acc_ref)

    def project(c):
        xc = xn_ref[c * CHUNK_TOKENS:(c + 1) * CHUNK_TOKENS, :]
        hu = lax.dot_general(u_ref[...], xc, (((1,), (1,)), ((), ())), preferred_element_type=F32)
        for j in range(per_chunk):
            hu_ref[c * per_chunk + j] = hu[:, j * LANES:(j + 1) * LANES]

    def weigh(c):
        for grp in range(c * per_chunk, (c + 1) * per_chunk):
            for i1 in range(eb // N_KEYS):
                row = lambda ref, h: jnp.broadcast_to(ref[h, grp, i1:i1 + 1, :], (ROWS_BF16, LANES))
                cnt = [row(cnt_ref, h) for h in range(PEER_HEADS)]
                e1 = [row(e1_ref, h) for h in range(PEER_HEADS)]
                for r in range(ROW_BLOCKS):
                    gate = jnp.zeros((ROWS_BF16, LANES), F32)
                    for h in range(PEER_HEADS):
                        base = (r * PEER_HEADS + h) * PAIR_ROWS
                        rank = sec_ref[grp, base:base + ROWS_BF16, :]
                        e2 = sec_ref[grp, base + ROWS_BF16:base + PAIR_ROWS, :]
                        gate = gate + jnp.where(rank < cnt[h], e1[h] * e2, 0.0)
                    erow = slice(i1 * N_KEYS + r * ROWS_BF16, i1 * N_KEYS + (r + 1) * ROWS_BF16)
                    x = hu_ref[grp, erow, :]
                    act = 0.5 * x * (1.0 + lax.erf(x * np.float32(2.0 ** -0.5)))
                    w_ref[grp, erow, :] = (act * gate).astype(BF16)

    def collect(c):
        w = jnp.concatenate([w_ref[c * per_chunk + j] for j in range(per_chunk)], axis=1)
        acc_ref[c] += jnp.dot(vt_ref[...], w, preferred_element_type=F32)

    project(0)
    for c in range(chunks):
        if c + 1 < chunks:
            project(c + 1)
        if c > 0:
            collect(c - 1)
        weigh(c)
    collect(chunks - 1)

    @pl.when(e == pl.num_programs(1) - 1)
    def _():
        for c in range(chunks):
            rows = slice(c * CHUNK_TOKENS, (c + 1) * CHUNK_TOKENS)
            y_ref[rows, :] = h_ref[rows, :] + acc_ref[c].T


def _experts(xn, h, u_bf, vt_bf, cnt, e1, sec, tb, eb):
    t = xn.shape[0]
    groups = tb // LANES
    tok = lambda n, **kw: pl.BlockSpec((tb, n), lambda i, j: (i, 0), **kw)
    once = dict(pipeline_mode=pl.Buffered(1))
    assert eb // N_KEYS == SUBLANES
    first = pl.BlockSpec((PEER_HEADS, groups, SUBLANES, LANES), lambda i, j: (0, i, j, 0))
    return pl.pallas_call(
        functools.partial(_experts_kernel, tb=tb, eb=eb),
        grid=(t // tb, N_EXPERTS // eb),
        in_specs=[tok(D_MODEL), tok(D_MODEL, **once),
                  pl.BlockSpec((eb, D_MODEL), lambda i, j: (j, 0)),
                  pl.BlockSpec((D_MODEL, eb), lambda i, j: (0, j)),
                  first, first, pl.BlockSpec((groups, SECOND_ROWS, LANES), lambda i, j: (i, 0, 0), **once)],
        out_specs=tok(D_MODEL),
        out_shape=jax.ShapeDtypeStruct((t, D_MODEL), F32),
        scratch_shapes=[pltpu.VMEM((groups, eb, LANES), F32), pltpu.VMEM((groups, eb, LANES), BF16),
                        pltpu.VMEM((tb // CHUNK_TOKENS, D_MODEL, CHUNK_TOKENS), F32)],
        compiler_params=_params("parallel", "arbitrary"),
        name="peer_experts",
    )(xn, h, u_bf, vt_bf, cnt, e1, sec)


def _q_perm():
    n = np.arange(ATTN_DIM)
    g, j, d = n // LANES, (n % LANES) // HEAD_DIM, n % HEAD_DIM
    return j * (GQA_GROUP * HEAD_DIM) + g * HEAD_DIM + d


def _segment_matrix():
    i = np.arange(QK_DIM)
    return (i[:, None] // HEAD_DIM == i[None, :] // HEAD_DIM).astype(np.float32)


def _sink_rows(sinks, s_rows):
    s = sinks.astype(F32).reshape(N_KV_HEADS, GQA_GROUP, 1, 1)
    return jnp.broadcast_to(s, (N_KV_HEADS, GQA_GROUP, s_rows, 1)).reshape(N_KV_HEADS, GQA_GROUP * s_rows, 1)


def _layer(l, xp, xs, cache_k, cache_v, state_conv, p):
    (norm1_g, w_in, q_norm_g, k_norm_g, attn_sinks, conv_w, conv_b, conv_ln_g, conv_ln_b, conv_pw_w,
     attn_out_g, conv_out_g, w_out, norm2_g, peer_wq, peer_keys, peer_u, peer_v) = [a[l] for a in p]
    bsz, seq, _ = xp.shape
    dbs, dseq, _ = xs.shape
    perm = _q_perm()

    w_in_bf = jnp.concatenate([w_in[:, :ATTN_DIM][:, perm], w_in[:, ATTN_DIM:]], axis=1).astype(BF16)
    seg = jnp.asarray(_segment_matrix(), BF16)
    gqk = jnp.concatenate([jnp.tile(q_norm_g, N_KV_HEADS * GQA_GROUP) * (HEAD_DIM ** -0.5),
                           jnp.tile(k_norm_g, N_KV_HEADS)])[None, :]
    g1 = norm1_g[None, :]
    ga = attn_out_g[perm][None, :]
    gc = conv_out_g[None, :]
    wa_bf = w_out[:ATTN_DIM][perm].astype(BF16)
    wc_bf = w_out[ATTN_DIM:].astype(BF16)
    g2 = norm2_g[None, :]
    cw = jnp.pad(conv_w, ((0, 1), (0, 0)))
    cb, lg, lb = conv_b[None, :], conv_ln_g[None, :], conv_ln_b[None, :]
    pw_bf = conv_pw_w.astype(BF16)
    wq_t_bf = peer_wq.T.astype(BF16)
    keys_bf = peer_keys.reshape(2 * PEER_HEADS, N_KEYS, PEER_HALF).astype(BF16)
    u_bf = peer_u.astype(BF16)
    vt_bf = peer_v.T.astype(BF16)

    def peer(xn, h):
        cnt, e1, sec = _scores(xn, wq_t_bf, keys_bf, 512)
        return _experts(xn, h, u_bf, vt_bf, cnt, e1, sec, 1024, SUBLANES * N_KEYS)

    tp = bsz * seq
    xpf = xp.reshape(tp, D_MODEL)
    q, k, v, glu = _proj(xpf, g1, w_in_bf, seg, gqk, 512)
    q3, k3, v3 = q.reshape(bsz, seq, ATTN_DIM), k.reshape(bsz, seq, KV_DIM), v.reshape(bsz, seq, KV_DIM)
    a = _attn_prompt(q3, k3, v3, _sink_rows(attn_sinks, WINDOW)).reshape(tp, ATTN_DIM)
    tile = 512
    glu4 = glu.reshape(bsz, seq // tile, tile, CONV_CH)
    hist = jnp.concatenate([jnp.zeros((bsz, 1, HIST_ROWS, CONV_CH), F32), glu4[:, :-1, -HIST_ROWS:]], axis=1)
    c = _conv(hist.reshape(-1, HIST_ROWS, CONV_CH), glu4.reshape(-1, tile, CONV_CH), cw, cb, lg, lb, pw_bf, 1, 32)
    h, xn = _merge(xpf, a, c, ga, gc, wa_bf, wc_bf, g2, 512)
    yp = peer(xn, h).reshape(bsz, seq, D_MODEL)
    pk = k3[:, -WINDOW:].reshape(bsz, WINDOW, N_KV_HEADS, HEAD_DIM)
    pv = v3[:, -WINDOW:].reshape(bsz, WINDOW, N_KV_HEADS, HEAD_DIM)
    pc = glu.reshape(bsz, seq, CONV_CH)[:, -(CONV_WIDTH - 1):]

    ts = dbs * dseq
    xsf = xs.reshape(ts, D_MODEL)
    q, k, v, glu = _proj(xsf, g1, w_in_bf, seg, gqk, 512)
    q3, k3, v3 = q.reshape(dbs, dseq, ATTN_DIM), k.reshape(dbs, dseq, KV_DIM), v.reshape(dbs, dseq, KV_DIM)
    ck = cache_k.reshape(dbs, -1, KV_DIM)
    cv = cache_v.reshape(dbs, -1, KV_DIM)
    a = _attn_sample(q3, ck, k3, cv, v3, _sink_rows(attn_sinks, dseq), 32).reshape(ts, ATTN_DIM)
    glu3 = glu.reshape(dbs, dseq, CONV_CH)
    hist = jnp.pad(state_conv, ((0, 0), (HIST_SKIP, 0), (0, 0)))
    c = _conv(hist, glu3, cw, cb, lg, lb, pw_bf, 64, dseq)
    h, xn = _merge(xsf, a, c, ga, gc, wa_bf, wc_bf, g2, 512)
    ys = peer(xn, h).reshape(dbs, dseq, D_MODEL)
    wb = ck.shape[1]
    sk = jnp.concatenate([ck, k3], axis=1)[:, -wb:].reshape(dbs, wb, N_KV_HEADS, HEAD_DIM)
    sv = jnp.concatenate([cv, v3], axis=1)[:, -wb:].reshape(dbs, wb, N_KV_HEADS, HEAD_DIM)
    sc = jnp.concatenate([state_conv, glu3], axis=1)[:, -(CONV_WIDTH - 1):]
    return yp, ys, (pk, pv, pc, sk, sv, sc)


def kernel(x_prompt, x_sample, cache_k, cache_v, state_conv, norm1_g, w_in, q_norm_g, k_norm_g, attn_sinks,
           conv_w, conv_b, conv_ln_g, conv_ln_b, conv_pw_w, attn_out_g, conv_out_g, w_out, norm2_g, peer_wq,
           peer_keys, peer_u, peer_v):
    params = (norm1_g, w_in, q_norm_g, k_norm_g, attn_sinks, conv_w, conv_b, conv_ln_g, conv_ln_b, conv_pw_w,
              attn_out_g, conv_out_g, w_out, norm2_g, peer_wq, peer_keys, peer_u, peer_v)
    depth = w_in.shape[0]
    xp, xs = x_prompt, x_sample
    states = []
    for l in range(depth):
        xp, xs, st = _layer(l, xp, xs, cache_k[l], cache_v[l], state_conv[l], params)
        states.append(st)
    stacked = [jnp.stack([st[i] for st in states]) for i in range(6)]
    return (xp, xs, *stacked)
```

```python
import functools

import numpy as np
import jax
import jax.numpy as jnp
from jax import lax
from jax.experimental import pallas as pl
from jax.experimental.pallas import tpu as pltpu

F32 = jnp.float32
BF16 = jnp.bfloat16

D_MODEL = 1024
ATTN_DIM = 512
CONV_CH = 512
HEAD_DIM = 64
N_KV_HEADS = 2
GQA_GROUP = 4
KV_DIM = N_KV_HEADS * HEAD_DIM
WINDOW = 128
CONV_WIDTH = 31
IN_DIM = ATTN_DIM + 2 * KV_DIM + 2 * CONV_CH
QK_DIM = ATTN_DIM + KV_DIM
PEER_HEADS = 8
N_KEYS = 128
N_EXPERTS = N_KEYS * N_KEYS
PEER_TOPK = 16
PEER_HALF = 128
EPS = 1e-6
NEG = -1e30

SUBLANES = 8
LANES = 128
HIST_ROWS = 32
HIST_SKIP = HIST_ROWS - (CONV_WIDTH - 1)

VMEM_LIMIT = 56 * 1024 * 1024


def _params(*sem):
    return pltpu.CompilerParams(dimension_semantics=sem, vmem_limit_bytes=VMEM_LIMIT)


def _proj_kernel(x_ref, g1_ref, w_ref, seg_ref, gqk_ref, q_ref, k_ref, v_ref, glu_ref):
    x = x_ref[...]
    xn = x * lax.rsqrt(jnp.mean(x * x, axis=-1, keepdims=True) + EPS) * g1_ref[...]
    z = jnp.dot(xn.astype(BF16), w_ref[...], preferred_element_type=F32)
    qk = z[:, :QK_DIM]
    sq = qk * qk
    hi = sq.astype(BF16)
    lo = (sq - hi.astype(F32)).astype(BF16)
    seg = seg_ref[...]
    ssum = jnp.dot(hi, seg, preferred_element_type=F32) + jnp.dot(lo, seg, preferred_element_type=F32)
    qkn = qk * lax.rsqrt(ssum * (1.0 / HEAD_DIM) + EPS) * gqk_ref[...]
    q_ref[...] = qkn[:, :ATTN_DIM]
    k_ref[...] = qkn[:, ATTN_DIM:]
    v_ref[...] = z[:, QK_DIM:QK_DIM + KV_DIM]
    val = z[:, QK_DIM + KV_DIM:QK_DIM + KV_DIM + CONV_CH]
    gate = z[:, QK_DIM + KV_DIM + CONV_CH:]
    glu_ref[...] = val * jax.nn.sigmoid(gate)


def _proj(x, g1, w_in_bf, seg, gqk, tm):
    t = x.shape[0]
    row = lambda n: pl.BlockSpec((tm, n), lambda i: (i, 0))
    full = lambda a: pl.BlockSpec(a.shape, lambda i: (0,) * a.ndim)
    return pl.pallas_call(
        _proj_kernel,
        grid=(t // tm,),
        in_specs=[row(D_MODEL), full(g1), full(w_in_bf), full(seg), full(gqk)],
        out_specs=[row(ATTN_DIM), row(KV_DIM), row(KV_DIM), row(CONV_CH)],
        out_shape=[jax.ShapeDtypeStruct((t, n), F32) for n in (ATTN_DIM, KV_DIM, KV_DIM, CONV_CH)],
        compiler_params=_params("parallel"),
        name="proj",
    )(x, g1, w_in_bf, seg, gqk)


def _attend(q, kk, vv, sink_ref, first):
    s_rows = q.shape[0]
    n_keys = WINDOW + s_rows
    qs = jnp.concatenate([q[:, c * LANES:(c + 1) * LANES] for c in range(GQA_GROUP)], axis=0)
    lane = lax.broadcasted_iota(jnp.int32, (1, LANES), 1)
    low = lane < HEAD_DIM
    qi = lax.broadcasted_iota(jnp.int32, (s_rows, n_keys), 0)
    kr = lax.broadcasted_iota(jnp.int32, (s_rows, n_keys), 1)
    vis = (kr > qi) & (kr <= qi + WINDOW)
    if first is not None:
        vis = vis & ((kr >= WINDOW) | jnp.logical_not(first))
    vis = jnp.concatenate([vis] * GQA_GROUP, axis=0)
    outs = []
    for j in range(N_KV_HEADS):
        qm = jnp.where(low if j == 0 else jnp.logical_not(low), qs, 0.0).astype(BF16)
        s = lax.dot_general(qm, kk, (((1,), (1,)), ((), ())), preferred_element_type=F32)
        s = jnp.where(vis, s, NEG)
        sink = sink_ref[j]
        m = jnp.maximum(jnp.max(s, axis=-1, keepdims=True), sink)
        p = jnp.exp(s - m)
        denom = jnp.sum(p, axis=-1, keepdims=True) + jnp.exp(sink - m)
        o = jnp.dot(p.astype(BF16), vv, preferred_element_type=F32)
        outs.append(o / denom)
    o = jnp.where(low, outs[0], outs[1])
    return [o[c * s_rows:(c + 1) * s_rows] for c in range(GQA_GROUP)]


def _attn_prompt_kernel(q_ref, kp_ref, kc_ref, vp_ref, vc_ref, sink_ref, a_ref, *, blocks):
    first = pl.program_id(1) == 0
    kk = jnp.concatenate([kp_ref[...], kc_ref[...]], axis=0).astype(BF16)
    vv = jnp.concatenate([vp_ref[...], vc_ref[...]], axis=0).astype(BF16)
    for b in range(blocks):
        rows = slice(b * WINDOW, (b + 1) * WINDOW)
        keys = slice(b * WINDOW, (b + 2) * WINDOW)
        cols = _attend(q_ref[rows, :], kk[keys], vv[keys], sink_ref, first if b == 0 else None)
        for c in range(GQA_GROUP):
            a_ref[rows, c * LANES:(c + 1) * LANES] = cols[c]


def _attn_prompt(q, k, v, sink_rows, blocks):
    b, t, _ = q.shape
    rows = blocks * WINDOW
    cur = lambda n: pl.BlockSpec((None, rows, n), lambda i, j: (i, j, 0))
    prev = lambda n: pl.BlockSpec((None, WINDOW, n), lambda i, j: (i, jnp.maximum(j * blocks - 1, 0), 0))
    return pl.pallas_call(
        functools.partial(_attn_prompt_kernel, blocks=blocks),
        grid=(b, t // rows),
        in_specs=[cur(ATTN_DIM), prev(KV_DIM), cur(KV_DIM), prev(KV_DIM), cur(KV_DIM),
                  pl.BlockSpec(sink_rows.shape, lambda i, j: (0, 0, 0))],
        out_specs=cur(ATTN_DIM),
        out_shape=jax.ShapeDtypeStruct((b, t, ATTN_DIM), F32),
        compiler_params=_params("parallel", "parallel"),
        name="attn_prompt",
    )(q, k, k, v, v, sink_rows)


def _attn_sample_kernel(q_ref, kp_ref, kc_ref, vp_ref, vc_ref, sink_ref, a_ref, *, seqs):
    def one(i, carry):
        kk = jnp.concatenate([kp_ref[i], kc_ref[i]], axis=0).astype(BF16)
        vv = jnp.concatenate([vp_ref[i], vc_ref[i]], axis=0).astype(BF16)
        cols = _attend(q_ref[i], kk, vv, sink_ref, None)
        a_ref[i] = jnp.concatenate(cols, axis=1)
        return carry
    lax.fori_loop(0, seqs, one, 0)


def _attn_sample(q, ck, k, cv, v, sink_rows, seqs):
    b, s, _ = q.shape
    blk = lambda r, n: pl.BlockSpec((seqs, r, n), lambda i: (i, 0, 0))
    return pl.pallas_call(
        functools.partial(_attn_sample_kernel, seqs=seqs),
        grid=(b // seqs,),
        in_specs=[blk(s, ATTN_DIM), blk(WINDOW, KV_DIM), blk(s, KV_DIM), blk(WINDOW, KV_DIM), blk(s, KV_DIM),
                  pl.BlockSpec(sink_rows.shape, lambda i: (0, 0, 0))],
        out_specs=blk(s, ATTN_DIM),
        out_shape=jax.ShapeDtypeStruct((b, s, ATTN_DIM), F32),
        compiler_params=_params("parallel"),
        name="attn_sample",
    )(q, ck, k, cv, v, sink_rows)


def _conv_kernel(hist_ref, cur_ref, w_ref, b_ref, lg_ref, lb_ref, pw_ref, out_ref, sh_ref, acc_ref,
                 *, seqs, rows, chunk):
    span = rows + HIST_ROWS - SUBLANES

    def one(i, carry):
        sh_ref[0, 0:HIST_ROWS, :] = hist_ref[i]
        sh_ref[0, HIST_ROWS:HIST_ROWS + rows, :] = cur_ref[i]
        for s in range(1, SUBLANES):
            sh_ref[s, 0:span, :] = sh_ref[0, s:s + span, :]
        base = pl.multiple_of(i * rows, SUBLANES)
        for c in range(rows // chunk):
            acc = jnp.broadcast_to(b_ref[...], (chunk, CONV_CH))
            for j in range(CONV_WIDTH):
                off = HIST_SKIP + j
                lo = c * chunk + off - off % SUBLANES
                acc = acc + sh_ref[off % SUBLANES, lo:lo + chunk, :] * w_ref[j:j + 1, :]
            acc_ref[pl.ds(base + c * chunk, chunk), :] = acc
        return carry
    lax.fori_loop(0, seqs, one, 0)
    c = acc_ref[...]
    mu = jnp.mean(c, axis=-1, keepdims=True)
    d = c - mu
    var = jnp.mean(d * d, axis=-1, keepdims=True)
    y = d * lax.rsqrt(var + EPS) * lg_ref[...] + lb_ref[...]
    y = y * jax.nn.sigmoid(y)
    out_ref[...] = jnp.dot(y.astype(BF16), pw_ref[...], preferred_element_type=F32)


def _conv(hist, cur, w, b, lg, lb, pw_bf, seqs, chunk):
    n, rows, _ = cur.shape
    full = lambda a: pl.BlockSpec(a.shape, lambda i: (0,) * a.ndim)
    return pl.pallas_call(
        functools.partial(_conv_kernel, seqs=seqs, rows=rows, chunk=chunk),
        grid=(n // seqs,),
        in_specs=[pl.BlockSpec((seqs, HIST_ROWS, CONV_CH), lambda i: (i, 0, 0)),
                  pl.BlockSpec((seqs, rows, CONV_CH), lambda i: (i, 0, 0)),
                  full(w), full(b), full(lg), full(lb), full(pw_bf)],
        out_specs=pl.BlockSpec((seqs * rows, CONV_CH), lambda i: (i, 0)),
        out_shape=jax.ShapeDtypeStruct((n * rows, CONV_CH), F32),
        scratch_shapes=[pltpu.VMEM((SUBLANES, HIST_ROWS + rows, CONV_CH), F32),
                        pltpu.VMEM((seqs * rows, CONV_CH), F32)],
        compiler_params=_params("parallel"),
        name="conv",
    )(hist, cur, w, b, lg, lb, pw_bf)


def _rms(x, g):
    return x * lax.rsqrt(jnp.mean(x * x, axis=-1, keepdims=True) + EPS) * g


def _merge_kernel(x_ref, a_ref, c_ref, ga_ref, gc_ref, wa_ref, wc_ref, g2_ref, h_ref, xn_ref):
    an = _rms(a_ref[...], ga_ref[...]).astype(BF16)
    cn = _rms(c_ref[...], gc_ref[...]).astype(BF16)
    h = (x_ref[...] + jnp.dot(an, wa_ref[...], preferred_element_type=F32)
         + jnp.dot(cn, wc_ref[...], preferred_element_type=F32))
    h_ref[...] = h
    xn_ref[...] = _rms(h, g2_ref[...]).astype(BF16)


def _merge(x, a, c, ga, gc, wa_bf, wc_bf, g2, tm):
    t = x.shape[0]
    row = lambda n: pl.BlockSpec((tm, n), lambda i: (i, 0))
    full = lambda a_: pl.BlockSpec(a_.shape, lambda i: (0,) * a_.ndim)
    return pl.pallas_call(
        _merge_kernel,
        grid=(t // tm,),
        in_specs=[row(D_MODEL), row(ATTN_DIM), row(CONV_CH), full(ga), full(gc), full(wa_bf), full(wc_bf), full(g2)],
        out_specs=[row(D_MODEL), row(D_MODEL)],
        out_shape=[jax.ShapeDtypeStruct((t, D_MODEL), F32), jax.ShapeDtypeStruct((t, D_MODEL), BF16)],
        compiler_params=_params("parallel"),
        name="merge",
    )(x, a, c, ga, gc, wa_bf, wc_bf, g2)


def _oddeven_merge_sort_pairs(n):
    pairs = []
    p = 1
    while p < n:
        k = p
        while k >= 1:
            for j in range(k % p, n - k, 2 * k):
                for i in range(min(k, n - j - k)):
                    if (i + j) // (2 * p) == (i + j + k) // (2 * p):
                        pairs.append((i + j, i + j + k))
            k //= 2
        p *= 2
    return pairs


_SORT16 = _oddeven_merge_sort_pairs(PEER_TOPK)


def _sort_desc(v):
    v = list(v)
    for i, j in _SORT16:
        v[i], v[j] = jnp.maximum(v[i], v[j]), jnp.minimum(v[i], v[j])
    return v


def _bitonic_to_desc(z):
    z = list(z)
    d = PEER_TOPK // 2
    while d >= 1:
        for i in range(PEER_TOPK):
            if i & d == 0:
                z[i], z[i + d] = jnp.maximum(z[i], z[i + d]), jnp.minimum(z[i], z[i + d])
        d //= 2
    return z


def _merge_top(x, y):
    return _bitonic_to_desc([jnp.maximum(x[i], y[PEER_TOPK - 1 - i]) for i in range(PEER_TOPK)])


def _across_sublanes(x):
    for shift in (4, 2, 1):
        x = _merge_top(x, [pltpu.roll(t, shift, 0) for t in x])
    return x


def _top16(tile):
    v = [tile[i * SUBLANES:(i + 1) * SUBLANES, :] for i in range(N_KEYS // SUBLANES)]
    return _across_sublanes(_sort_desc(v))


def _best_sums(a, b):
    sub = lax.broadcasted_iota(jnp.int32, (SUBLANES, LANES), 0)
    a_lo, a_hi = a[0], a[SUBLANES]
    for r in range(1, SUBLANES):
        a_lo = jnp.where(sub == r, a[r], a_lo)
        a_hi = jnp.where(sub == r, a[SUBLANES + r], a_hi)
    x = [jnp.where(sub < PEER_TOPK // (j + 1), a_lo + b[j], -jnp.inf) for j in range(PEER_TOPK)]
    x[PEER_TOPK - 1] = jnp.maximum(x[PEER_TOPK - 1], a_hi + b[0])
    return _across_sublanes(_bitonic_to_desc(x))


def _first_key_cuts(a, b, thr):
    cuts = []
    for j in range(PEER_TOPK):
        cut = jnp.full_like(thr, jnp.inf)
        for k in range(PEER_TOPK // (j + 1)):
            cut = jnp.where(a[k] + b[j] >= thr, a[k], cut)
        cuts.append(cut)
    return cuts


ROWS_BF16 = 2 * SUBLANES
ROW_BLOCKS = N_KEYS // ROWS_BF16
PAIR_ROWS = 2 * ROWS_BF16
SECOND_ROWS = ROW_BLOCKS * PEER_HEADS * PAIR_ROWS


def _scores_kernel(xn_ref, wq_ref, keys_ref, cnt_ref, e1_ref, sec_ref, s_ref, *, tb):
    xn = xn_ref[...]
    for hp in range(2 * PEER_HEADS):
        qt = lax.dot_general(wq_ref[hp * PEER_HALF:(hp + 1) * PEER_HALF, :], xn, (((1,), (1,)), ((), ())),
                             preferred_element_type=F32)
        s_ref[hp] = jnp.dot(keys_ref[hp], qt.astype(BF16), preferred_element_type=F32)

    groups = tb // LANES
    tiles = N_KEYS // SUBLANES

    def one(it, carry):
        h = it // groups
        grp = it % groups
        lanes = pl.ds(pl.multiple_of(grp * LANES, LANES), LANES)
        s1 = s_ref[2 * h, :, lanes]
        s2 = s_ref[2 * h + 1, :, lanes]
        a = _top16(s1)
        b = _top16(s2)
        c = _best_sums(a, b)
        z = jnp.ones_like(c[0])
        for k in range(1, PEER_TOPK):
            z = z + jnp.exp(c[k] - c[0])
        inv_z = 0.5 / z
        cuts = _first_key_cuts(a, b, c[PEER_TOPK - 1])
        cnt, e1, rank, e2 = [], [], [], []
        for i in range(tiles):
            x1 = s1[i * SUBLANES:(i + 1) * SUBLANES, :]
            x2 = s2[i * SUBLANES:(i + 1) * SUBLANES, :]
            n = jnp.zeros_like(x1)
            r = jnp.zeros_like(x2)
            for k in range(PEER_TOPK):
                n = jnp.where(x1 >= cuts[k], k + 1.0, n)
                r = jnp.where(b[k] > x2, k + 1.0, r)
            cnt.append(n)
            rank.append(r)
            e1.append(jnp.exp(x1 - a[0]) * inv_z)
            e2.append(jnp.exp(x2 - b[0]))
        cnt_ref[h, grp] = jnp.concatenate(cnt, axis=0)
        e1_ref[h, grp] = jnp.concatenate(e1, axis=0)
        for r in range(ROW_BLOCKS):
            row = pl.multiple_of((r * PEER_HEADS + h) * PAIR_ROWS, PAIR_ROWS)
            pair = jnp.concatenate(rank[2 * r:2 * r + 2] + e2[2 * r:2 * r + 2], axis=0)
            sec_ref[grp, pl.ds(row, PAIR_ROWS), :] = pair
        return carry

    lax.fori_loop(0, PEER_HEADS * groups, one, 0)


def _scores(xn, wq_t_bf, keys_bf, tb):
    t = xn.shape[0]
    groups = tb // LANES
    first = pl.BlockSpec((PEER_HEADS, groups, N_KEYS, LANES), lambda i: (0, i, 0, 0))
    return pl.pallas_call(
        functools.partial(_scores_kernel, tb=tb),
        grid=(t // tb,),
        in_specs=[pl.BlockSpec((tb, D_MODEL), lambda i: (i, 0)),
                  pl.BlockSpec(wq_t_bf.shape, lambda i: (0, 0)),
                  pl.BlockSpec(keys_bf.shape, lambda i: (0, 0, 0))],
        out_specs=[first, first, pl.BlockSpec((groups, SECOND_ROWS, LANES), lambda i: (i, 0, 0))],
        out_shape=[jax.ShapeDtypeStruct((PEER_HEADS, t // LANES, N_KEYS, LANES), F32)] * 2
                  + [jax.ShapeDtypeStruct((t // LANES, SECOND_ROWS, LANES), F32)],
        scratch_shapes=[pltpu.VMEM((2 * PEER_HEADS, N_KEYS, tb), F32)],
        compiler_params=_params("parallel"),
        name="peer_scores",
    )(xn, wq_t_bf, keys_bf)


CHUNK_TOKENS = 256


def _experts_kernel(xn_ref, h_ref, u_ref, vt_ref, cnt_ref, e1_ref, sec_ref, y_ref, hu_ref, w_ref, acc_ref,
                    *, tb, eb):
    e = pl.program_id(1)
    chunks = tb // CHUNK_TOKENS
    per_chunk = CHUNK_TOKENS // LANES

    @pl.when(e == 0)
    def _():
        acc_ref[...] = jnp.zeros_like(acc_ref)

    def project(c):
        xc = xn_ref[c * CHUNK_TOKENS:(c + 1) * CHUNK_TOKENS, :]
        hu = lax.dot_general(u_ref[...], xc, (((1,), (1,)), ((), ())), preferred_element_type=F32)
        for j in range(per_chunk):
            hu_ref[c * per_chunk + j] = hu[:, j * LANES:(j + 1) * LANES]

    def weigh(c):
        for grp in range(c * per_chunk, (c + 1) * per_chunk):
            for i1 in range(eb // N_KEYS):
                row = lambda ref, h: jnp.broadcast_to(ref[h, grp, i1:i1 + 1, :], (ROWS_BF16, LANES))
                cnt = [row(cnt_ref, h) for h in range(PEER_HEADS)]
                e1 = [row(e1_ref, h) for h in range(PEER_HEADS)]
                for r in range(ROW_BLOCKS):
                    gate = jnp.zeros((ROWS_BF16, LANES), F32)
                    for h in range(PEER_HEADS):
                        base = (r * PEER_HEADS + h) * PAIR_ROWS
                        rank = sec_ref[grp, base:base + ROWS_BF16, :]
                        e2 = sec_ref[grp, base + ROWS_BF16:base + PAIR_ROWS, :]
                        gate = gate + jnp.where(rank < cnt[h], e1[h] * e2, 0.0)
                    erow = slice(i1 * N_KEYS + r * ROWS_BF16, i1 * N_KEYS + (r + 1) * ROWS_BF16)
                    x = hu_ref[grp, erow, :]
                    act = x * (1.0 + lax.erf(x * np.float32(2.0 ** -0.5)))
                    w_ref[grp, erow, :] = (act * gate).astype(BF16)

    def collect(c):
        w = jnp.concatenate([w_ref[c * per_chunk + j] for j in range(per_chunk)], axis=1)
        acc_ref[c] += jnp.dot(vt_ref[...], w, preferred_element_type=F32)

    project(0)
    for c in range(chunks):
        if c + 1 < chunks:
            project(c + 1)
        if c > 0:
            collect(c - 1)
        weigh(c)
    collect(chunks - 1)

    @pl.when(e == pl.num_programs(1) - 1)
    def _():
        for c in range(chunks):
            rows = slice(c * CHUNK_TOKENS, (c + 1) * CHUNK_TOKENS)
            y_ref[rows, :] = h_ref[rows, :] + acc_ref[c].T


def _experts(xn, h, u_bf, vt_bf, cnt, e1, sec, tb, eb):
    t = xn.shape[0]
    groups = tb // LANES
    tok = lambda n, **kw: pl.BlockSpec((tb, n), lambda i, j: (i, 0), **kw)
    once = dict(pipeline_mode=pl.Buffered(1))
    assert eb // N_KEYS == SUBLANES
    first = pl.BlockSpec((PEER_HEADS, groups, SUBLANES, LANES), lambda i, j: (0, i, j, 0))
    return pl.pallas_call(
        functools.partial(_experts_kernel, tb=tb, eb=eb),
        grid=(t // tb, N_EXPERTS // eb),
        in_specs=[tok(D_MODEL), tok(D_MODEL, **once),
                  pl.BlockSpec((eb, D_MODEL), lambda i, j: (j, 0)),
                  pl.BlockSpec((D_MODEL, eb), lambda i, j: (0, j)),
                  first, first, pl.BlockSpec((groups, SECOND_ROWS, LANES), lambda i, j: (i, 0, 0), **once)],
        out_specs=tok(D_MODEL),
        out_shape=jax.ShapeDtypeStruct((t, D_MODEL), F32),
        scratch_shapes=[pltpu.VMEM((groups, eb, LANES), F32), pltpu.VMEM((groups, eb, LANES), BF16),
                        pltpu.VMEM((tb // CHUNK_TOKENS, D_MODEL, CHUNK_TOKENS), F32)],
        compiler_params=_params("parallel", "arbitrary"),
        name="peer_experts",
    )(xn, h, u_bf, vt_bf, cnt, e1, sec)


def _q_perm():
    n = np.arange(ATTN_DIM)
    g, j, d = n // LANES, (n % LANES) // HEAD_DIM, n % HEAD_DIM
    return j * (GQA_GROUP * HEAD_DIM) + g * HEAD_DIM + d


def _segment_matrix():
    i = np.arange(QK_DIM)
    return (i[:, None] // HEAD_DIM == i[None, :] // HEAD_DIM).astype(np.float32)


def _sink_rows(sinks, s_rows):
    s = sinks.astype(F32).reshape(N_KV_HEADS, GQA_GROUP, 1, 1)
    return jnp.broadcast_to(s, (N_KV_HEADS, GQA_GROUP, s_rows, 1)).reshape(N_KV_HEADS, GQA_GROUP * s_rows, 1)


def _layer(l, xp, xs, cache_k, cache_v, state_conv, p):
    (norm1_g, w_in, q_norm_g, k_norm_g, attn_sinks, conv_w, conv_b, conv_ln_g, conv_ln_b, conv_pw_w,
     attn_out_g, conv_out_g, w_out, norm2_g, peer_wq, peer_keys, peer_u, peer_v) = [a[l] for a in p]
    bsz, seq, _ = xp.shape
    dbs, dseq, _ = xs.shape
    perm = _q_perm()

    w_in_bf = jnp.concatenate([w_in[:, :ATTN_DIM][:, perm], w_in[:, ATTN_DIM:]], axis=1).astype(BF16)
    seg = jnp.asarray(_segment_matrix(), BF16)
    gqk = jnp.concatenate([jnp.tile(q_norm_g, N_KV_HEADS * GQA_GROUP) * (HEAD_DIM ** -0.5),
                           jnp.tile(k_norm_g, N_KV_HEADS)])[None, :]
    g1 = norm1_g[None, :]
    ga = attn_out_g[perm][None, :]
    gc = conv_out_g[None, :]
    wa_bf = w_out[:ATTN_DIM][perm].astype(BF16)
    wc_bf = w_out[ATTN_DIM:].astype(BF16)
    g2 = norm2_g[None, :]
    cw = jnp.pad(conv_w, ((0, 1), (0, 0)))
    cb, lg, lb = conv_b[None, :], conv_ln_g[None, :], conv_ln_b[None, :]
    pw_bf = conv_pw_w.astype(BF16)
    wq_t_bf = peer_wq.T.astype(BF16)
    keys_bf = peer_keys.reshape(2 * PEER_HEADS, N_KEYS, PEER_HALF).astype(BF16)
    u_bf = peer_u.astype(BF16)
    vt_bf = peer_v.T.astype(BF16)

    def peer(xn, h):
        cnt, e1, sec = _scores(xn, wq_t_bf, keys_bf, 512)
        return _experts(xn, h, u_bf, vt_bf, cnt, e1, sec, 1024, SUBLANES * N_KEYS)

    tp = bsz * seq
    xpf = xp.reshape(tp, D_MODEL)
    q, k, v, glu = _proj(xpf, g1, w_in_bf, seg, gqk, 512)
    q3, k3, v3 = q.reshape(bsz, seq, ATTN_DIM), k.reshape(bsz, seq, KV_DIM), v.reshape(bsz, seq, KV_DIM)
    a = _attn_prompt(q3, k3, v3, _sink_rows(attn_sinks, WINDOW), 4).reshape(tp, ATTN_DIM)
    tile = 512
    glu4 = glu.reshape(bsz, seq // tile, tile, CONV_CH)
    hist = jnp.concatenate([jnp.zeros((bsz, 1, HIST_ROWS, CONV_CH), F32), glu4[:, :-1, -HIST_ROWS:]], axis=1)
    c = _conv(hist.reshape(-1, HIST_ROWS, CONV_CH), glu4.reshape(-1, tile, CONV_CH), cw, cb, lg, lb, pw_bf, 1, 32)
    h, xn = _merge(xpf, a, c, ga, gc, wa_bf, wc_bf, g2, 512)
    yp = peer(xn, h).reshape(bsz, seq, D_MODEL)
    pk = k3[:, -WINDOW:].reshape(bsz, WINDOW, N_KV_HEADS, HEAD_DIM)
    pv = v3[:, -WINDOW:].reshape(bsz, WINDOW, N_KV_HEADS, HEAD_DIM)
    pc = glu.reshape(bsz, seq, CONV_CH)[:, -(CONV_WIDTH - 1):]

    ts = dbs * dseq
    xsf = xs.reshape(ts, D_MODEL)
    q, k, v, glu = _proj(xsf, g1, w_in_bf, seg, gqk, 512)
    q3, k3, v3 = q.reshape(dbs, dseq, ATTN_DIM), k.reshape(dbs, dseq, KV_DIM), v.reshape(dbs, dseq, KV_DIM)
    ck = cache_k.reshape(dbs, -1, KV_DIM)
    cv = cache_v.reshape(dbs, -1, KV_DIM)
    a = _attn_sample(q3, ck, k3, cv, v3, _sink_rows(attn_sinks, dseq), 32).reshape(ts, ATTN_DIM)
    glu3 = glu.reshape(dbs, dseq, CONV_CH)
    hist = jnp.pad(state_conv, ((0, 0), (HIST_SKIP, 0), (0, 0)))
    c = _conv(hist, glu3, cw, cb, lg, lb, pw_bf, 64, dseq)
    h, xn = _merge(xsf, a, c, ga, gc, wa_bf, wc_bf, g2, 512)
    ys = peer(xn, h).reshape(dbs, dseq, D_MODEL)
    wb = ck.shape[1]
    sk = jnp.concatenate([ck, k3], axis=1)[:, -wb:].reshape(dbs, wb, N_KV_HEADS, HEAD_DIM)
    sv = jnp.concatenate([cv, v3], axis=1)[:, -wb:].reshape(dbs, wb, N_KV_HEADS, HEAD_DIM)
    sc = jnp.concatenate([state_conv, glu3], axis=1)[:, -(CONV_WIDTH - 1):]
    return yp, ys, (pk, pv, pc, sk, sv, sc)


def kernel(x_prompt, x_sample, cache_k, cache_v, state_conv, norm1_g, w_in, q_norm_g, k_norm_g, attn_sinks,
           conv_w, conv_b, conv_ln_g, conv_ln_b, conv_pw_w, attn_out_g, conv_out_g, w_out, norm2_g, peer_wq,
           peer_keys, peer_u, peer_v):
    params = (norm1_g, w_in, q_norm_g, k_norm_g, attn_sinks, conv_w, conv_b, conv_ln_g, conv_ln_b, conv_pw_w,
              attn_out_g, conv_out_g, w_out, norm2_g, peer_wq, peer_keys, peer_u, peer_v)
    depth = w_in.shape[0]
    xp, xs = x_prompt, x_sample
    states = []
    for l in range(depth):
        xp, xs, st = _layer(l, xp, xs, cache_k[l], cache_v[l], state_conv[l], params)
        states.append(st)
    stacked = [jnp.stack([st[i] for st in states]) for i in range(6)]
    return (xp, xs, *stacked)
```

```python
import functools

import numpy as np
import jax
import jax.numpy as jnp
from jax import lax
from jax.experimental import pallas as pl
from jax.experimental.pallas import tpu as pltpu

F32 = jnp.float32
BF16 = jnp.bfloat16

D_MODEL = 1024
ATTN_DIM = 512
CONV_CH = 512
HEAD_DIM = 64
N_KV_HEADS = 2
GQA_GROUP = 4
KV_DIM = N_KV_HEADS * HEAD_DIM
WINDOW = 128
CONV_WIDTH = 31
IN_DIM = ATTN_DIM + 2 * KV_DIM + 2 * CONV_CH
QK_DIM = ATTN_DIM + KV_DIM
PEER_HEADS = 8
N_KEYS = 128
N_EXPERTS = N_KEYS * N_KEYS
PEER_TOPK = 16
PEER_HALF = 128
EPS = 1e-6
NEG = -1e30

SUBLANES = 8
LANES = 128
HIST_ROWS = 32
HIST_SKIP = HIST_ROWS - (CONV_WIDTH - 1)

VMEM_LIMIT = 56 * 1024 * 1024


def _params(*sem, **kw):
    return pltpu.CompilerParams(dimension_semantics=sem, vmem_limit_bytes=VMEM_LIMIT, **kw)


def _proj_kernel(x_ref, g1_ref, w_ref, seg_ref, gqk_ref, q_ref, k_ref, v_ref, glu_ref):
    x = x_ref[...]
    xn = x * lax.rsqrt(jnp.mean(x * x, axis=-1, keepdims=True) + EPS) * g1_ref[...]
    z = jnp.dot(xn.astype(BF16), w_ref[...], preferred_element_type=F32)
    qk = z[:, :QK_DIM]
    sq = qk * qk
    hi = sq.astype(BF16)
    lo = (sq - hi.astype(F32)).astype(BF16)
    seg = seg_ref[...]
    ssum = jnp.dot(hi, seg, preferred_element_type=F32) + jnp.dot(lo, seg, preferred_element_type=F32)
    qkn = qk * lax.rsqrt(ssum * (1.0 / HEAD_DIM) + EPS) * gqk_ref[...]
    q_ref[...] = qkn[:, :ATTN_DIM]
    k_ref[...] = qkn[:, ATTN_DIM:]
    v_ref[...] = z[:, QK_DIM:QK_DIM + KV_DIM]
    val = z[:, QK_DIM + KV_DIM:QK_DIM + KV_DIM + CONV_CH]
    gate = z[:, QK_DIM + KV_DIM + CONV_CH:]
    glu_ref[...] = val * jax.nn.sigmoid(gate)


def _proj(x, g1, w_in_bf, seg, gqk, tm):
    t = x.shape[0]
    row = lambda n: pl.BlockSpec((tm, n), lambda i: (i, 0))
    full = lambda a: pl.BlockSpec(a.shape, lambda i: (0,) * a.ndim)
    return pl.pallas_call(
        _proj_kernel,
        grid=(t // tm,),
        in_specs=[row(D_MODEL), full(g1), full(w_in_bf), full(seg), full(gqk)],
        out_specs=[row(ATTN_DIM), row(KV_DIM), row(KV_DIM), row(CONV_CH)],
        out_shape=[jax.ShapeDtypeStruct((t, n), F32) for n in (ATTN_DIM, KV_DIM, KV_DIM, CONV_CH)],
        compiler_params=_params("parallel"),
        name="proj",
    )(x, g1, w_in_bf, seg, gqk)


def _attend(q, kk, vv, sink_ref, first):
    s_rows = q.shape[0]
    n_keys = WINDOW + s_rows
    qs = jnp.concatenate([q[:, c * LANES:(c + 1) * LANES] for c in range(GQA_GROUP)], axis=0)
    lane = lax.broadcasted_iota(jnp.int32, (1, LANES), 1)
    low = lane < HEAD_DIM
    qi = lax.broadcasted_iota(jnp.int32, (s_rows, n_keys), 0)
    kr = lax.broadcasted_iota(jnp.int32, (s_rows, n_keys), 1)
    vis = (kr > qi) & (kr <= qi + WINDOW)
    if first is not None:
        vis = vis & ((kr >= WINDOW) | jnp.logical_not(first))
    vis = jnp.concatenate([vis] * GQA_GROUP, axis=0)
    outs = []
    for j in range(N_KV_HEADS):
        qm = jnp.where(low if j == 0 else jnp.logical_not(low), qs, 0.0).astype(BF16)
        s = lax.dot_general(qm, kk, (((1,), (1,)), ((), ())), preferred_element_type=F32)
        s = jnp.where(vis, s, NEG)
        sink = sink_ref[j]
        m = jnp.maximum(jnp.max(s, axis=-1, keepdims=True), sink)
        p = jnp.exp(s - m)
        denom = jnp.sum(p, axis=-1, keepdims=True) + jnp.exp(sink - m)
        o = jnp.dot(p.astype(BF16), vv, preferred_element_type=F32)
        outs.append(o / denom)
    o = jnp.where(low, outs[0], outs[1])
    return [o[c * s_rows:(c + 1) * s_rows] for c in range(GQA_GROUP)]


def _attn_prompt_kernel(q_ref, kp_ref, kc_ref, vp_ref, vc_ref, sink_ref, a_ref, *, blocks):
    first = pl.program_id(1) == 0
    kk = jnp.concatenate([kp_ref[...], kc_ref[...]], axis=0).astype(BF16)
    vv = jnp.concatenate([vp_ref[...], vc_ref[...]], axis=0).astype(BF16)
    for b in range(blocks):
        rows = slice(b * WINDOW, (b + 1) * WINDOW)
        keys = slice(b * WINDOW, (b + 2) * WINDOW)
        cols = _attend(q_ref[rows, :], kk[keys], vv[keys], sink_ref, first if b == 0 else None)
        for c in range(GQA_GROUP):
            a_ref[rows, c * LANES:(c + 1) * LANES] = cols[c]


def _attn_prompt(q, k, v, sink_rows, blocks):
    b, t, _ = q.shape
    rows = blocks * WINDOW
    cur = lambda n: pl.BlockSpec((None, rows, n), lambda i, j: (i, j, 0))
    prev = lambda n: pl.BlockSpec((None, WINDOW, n), lambda i, j: (i, jnp.maximum(j * blocks - 1, 0), 0))
    return pl.pallas_call(
        functools.partial(_attn_prompt_kernel, blocks=blocks),
        grid=(b, t // rows),
        in_specs=[cur(ATTN_DIM), prev(KV_DIM), cur(KV_DIM), prev(KV_DIM), cur(KV_DIM),
                  pl.BlockSpec(sink_rows.shape, lambda i, j: (0, 0, 0))],
        out_specs=cur(ATTN_DIM),
        out_shape=jax.ShapeDtypeStruct((b, t, ATTN_DIM), F32),
        compiler_params=_params("parallel", "parallel"),
        name="attn_prompt",
    )(q, k, k, v, v, sink_rows)


def _attn_sample_kernel(q_ref, kp_ref, kc_ref, vp_ref, vc_ref, sink_ref, a_ref, *, seqs):
    def one(i, carry):
        kk = jnp.concatenate([kp_ref[i], kc_ref[i]], axis=0).astype(BF16)
        vv = jnp.concatenate([vp_ref[i], vc_ref[i]], axis=0).astype(BF16)
        cols = _attend(q_ref[i], kk, vv, sink_ref, None)
        a_ref[i] = jnp.concatenate(cols, axis=1)
        return carry
    lax.fori_loop(0, seqs, one, 0)


def _attn_sample(q, ck, k, cv, v, sink_rows, seqs):
    b, s, _ = q.shape
    blk = lambda r, n: pl.BlockSpec((seqs, r, n), lambda i: (i, 0, 0))
    return pl.pallas_call(
        functools.partial(_attn_sample_kernel, seqs=seqs),
        grid=(b // seqs,),
        in_specs=[blk(s, ATTN_DIM), blk(WINDOW, KV_DIM), blk(s, KV_DIM), blk(WINDOW, KV_DIM), blk(s, KV_DIM),
                  pl.BlockSpec(sink_rows.shape, lambda i: (0, 0, 0))],
        out_specs=blk(s, ATTN_DIM),
        out_shape=jax.ShapeDtypeStruct((b, s, ATTN_DIM), F32),
        compiler_params=_params("parallel"),
        name="attn_sample",
    )(q, ck, k, cv, v, sink_rows)


def _conv_kernel(hist_ref, cur_ref, w_ref, b_ref, lg_ref, lb_ref, pw_ref, out_ref, sh_ref, acc_ref,
                 *, seqs, rows, chunk):
    span = rows + HIST_ROWS - SUBLANES

    def one(i, carry):
        sh_ref[0, 0:HIST_ROWS, :] = hist_ref[i]
        sh_ref[0, HIST_ROWS:HIST_ROWS + rows, :] = cur_ref[i]
        for s in range(1, SUBLANES):
            sh_ref[s, 0:span, :] = sh_ref[0, s:s + span, :]
        base = pl.multiple_of(i * rows, SUBLANES)
        for c in range(rows // chunk):
            acc = jnp.broadcast_to(b_ref[...], (chunk, CONV_CH))
            for j in range(CONV_WIDTH):
                off = HIST_SKIP + j
                lo = c * chunk + off - off % SUBLANES
                acc = acc + sh_ref[off % SUBLANES, lo:lo + chunk, :] * w_ref[j:j + 1, :]
            acc_ref[pl.ds(base + c * chunk, chunk), :] = acc
        return carry
    lax.fori_loop(0, seqs, one, 0)
    c = acc_ref[...]
    mu = jnp.mean(c, axis=-1, keepdims=True)
    d = c - mu
    var = jnp.mean(d * d, axis=-1, keepdims=True)
    y = d * lax.rsqrt(var + EPS) * lg_ref[...] + lb_ref[...]
    y = y * jax.nn.sigmoid(y)
    out_ref[...] = jnp.dot(y.astype(BF16), pw_ref[...], preferred_element_type=F32)


def _conv(hist, cur, w, b, lg, lb, pw_bf, seqs, chunk):
    n, rows, _ = cur.shape
    full = lambda a: pl.BlockSpec(a.shape, lambda i: (0,) * a.ndim)
    return pl.pallas_call(
        functools.partial(_conv_kernel, seqs=seqs, rows=rows, chunk=chunk),
        grid=(n // seqs,),
        in_specs=[pl.BlockSpec((seqs, HIST_ROWS, CONV_CH), lambda i: (i, 0, 0)),
                  pl.BlockSpec((seqs, rows, CONV_CH), lambda i: (i, 0, 0)),
                  full(w), full(b), full(lg), full(lb), full(pw_bf)],
        out_specs=pl.BlockSpec((seqs * rows, CONV_CH), lambda i: (i, 0)),
        out_shape=jax.ShapeDtypeStruct((n * rows, CONV_CH), F32),
        scratch_shapes=[pltpu.VMEM((SUBLANES, HIST_ROWS + rows, CONV_CH), F32),
                        pltpu.VMEM((seqs * rows, CONV_CH), F32)],
        compiler_params=_params("parallel"),
        name="conv",
    )(hist, cur, w, b, lg, lb, pw_bf)


def _rms(x, g):
    return x * lax.rsqrt(jnp.mean(x * x, axis=-1, keepdims=True) + EPS) * g


def _merge_kernel(x_ref, a_ref, c_ref, ga_ref, gc_ref, wa_ref, wc_ref, g2_ref, h_ref, xn_ref):
    an = _rms(a_ref[...], ga_ref[...]).astype(BF16)
    cn = _rms(c_ref[...], gc_ref[...]).astype(BF16)
    h = (x_ref[...] + jnp.dot(an, wa_ref[...], preferred_element_type=F32)
         + jnp.dot(cn, wc_ref[...], preferred_element_type=F32))
    h_ref[...] = h
    xn_ref[...] = _rms(h, g2_ref[...]).T.astype(BF16)


def _merge(x, a, c, ga, gc, wa_bf, wc_bf, g2, tm):
    t = x.shape[0]
    row = lambda n: pl.BlockSpec((tm, n), lambda i: (i, 0))
    full = lambda a_: pl.BlockSpec(a_.shape, lambda i: (0,) * a_.ndim)
    return pl.pallas_call(
        _merge_kernel,
        grid=(t // tm,),
        in_specs=[row(D_MODEL), row(ATTN_DIM), row(CONV_CH), full(ga), full(gc), full(wa_bf), full(wc_bf), full(g2)],
        out_specs=[row(D_MODEL), pl.BlockSpec((D_MODEL, tm), lambda i: (0, i))],
        out_shape=[jax.ShapeDtypeStruct((t, D_MODEL), F32), jax.ShapeDtypeStruct((D_MODEL, t), BF16)],
        compiler_params=_params("parallel"),
        name="merge",
    )(x, a, c, ga, gc, wa_bf, wc_bf, g2)


def _oddeven_merge_sort_pairs(n):
    pairs = []
    p = 1
    while p < n:
        k = p
        while k >= 1:
            for j in range(k % p, n - k, 2 * k):
                for i in range(min(k, n - j - k)):
                    if (i + j) // (2 * p) == (i + j + k) // (2 * p):
                        pairs.append((i + j, i + j + k))
            k //= 2
        p *= 2
    return pairs


_SORT16 = _oddeven_merge_sort_pairs(PEER_TOPK)


def _sort_desc(v):
    v = list(v)
    for i, j in _SORT16:
        v[i], v[j] = jnp.maximum(v[i], v[j]), jnp.minimum(v[i], v[j])
    return v


def _bitonic_to_desc(z):
    z = list(z)
    d = PEER_TOPK // 2
    while d >= 1:
        for i in range(PEER_TOPK):
            if i & d == 0:
                z[i], z[i + d] = jnp.maximum(z[i], z[i + d]), jnp.minimum(z[i], z[i + d])
        d //= 2
    return z


def _merge_top(x, y):
    return _bitonic_to_desc([jnp.maximum(x[i], y[PEER_TOPK - 1 - i]) for i in range(PEER_TOPK)])


def _across_sublanes(x):
    for shift in (4, 2, 1):
        x = _merge_top(x, [pltpu.roll(t, shift, 0) for t in x])
    return x


def _top16(tile):
    v = [tile[i * SUBLANES:(i + 1) * SUBLANES, :] for i in range(N_KEYS // SUBLANES)]
    return _across_sublanes(_sort_desc(v))


def _best_sums(a, b):
    sub = lax.broadcasted_iota(jnp.int32, (SUBLANES, LANES), 0)
    a_lo, a_hi = a[0], a[SUBLANES]
    for r in range(1, SUBLANES):
        a_lo = jnp.where(sub == r, a[r], a_lo)
        a_hi = jnp.where(sub == r, a[SUBLANES + r], a_hi)
    x = [jnp.where(sub < PEER_TOPK // (j + 1), a_lo + b[j], -jnp.inf) for j in range(PEER_TOPK)]
    x[PEER_TOPK - 1] = jnp.maximum(x[PEER_TOPK - 1], a_hi + b[0])
    return _across_sublanes(_bitonic_to_desc(x))


def _first_key_cuts(a, b, thr):
    cuts = []
    for j in range(PEER_TOPK):
        cut = jnp.full_like(thr, jnp.inf)
        for k in range(PEER_TOPK // (j + 1)):
            cut = jnp.where(a[k] + b[j] >= thr, a[k], cut)
        cuts.append(cut)
    return cuts


ROWS_BF16 = 2 * SUBLANES
ROW_BLOCKS = N_KEYS // ROWS_BF16
PAIR_ROWS = 2 * ROWS_BF16
SECOND_ROWS = ROW_BLOCKS * PEER_HEADS * PAIR_ROWS


def _scores_kernel(xn_ref, wq_ref, keys_ref, cnt_ref, e1_ref, sec_ref, s_ref, *, tb):
    groups = tb // LANES
    tiles = N_KEYS // SUBLANES

    def head_scores(h, slot):
        for p in range(2):
            rows = pl.ds(pl.multiple_of((2 * h + p) * PEER_HALF, PEER_HALF), PEER_HALF)
            qt = jnp.dot(wq_ref[rows, :], xn_ref[...], preferred_element_type=F32)
            s_ref[slot, p] = jnp.dot(keys_ref[2 * h + p], qt.astype(BF16), preferred_element_type=F32)

    def one(h, slot, grp):
        lanes = slice(grp * LANES, (grp + 1) * LANES)
        s1 = s_ref[slot, 0, :, lanes]
        s2 = s_ref[slot, 1, :, lanes]
        a = _top16(s1)
        b = _top16(s2)
        c = _best_sums(a, b)
        z = jnp.ones_like(c[0])
        for k in range(1, PEER_TOPK):
            z = z + jnp.exp(c[k] - c[0])
        inv_z = 0.5 / z
        cuts = _first_key_cuts(a, b, c[PEER_TOPK - 1])
        cnt, e1, rank, e2 = [], [], [], []
        for i in range(tiles):
            x1 = s1[i * SUBLANES:(i + 1) * SUBLANES, :]
            x2 = s2[i * SUBLANES:(i + 1) * SUBLANES, :]
            n = jnp.zeros_like(x1)
            r = jnp.zeros_like(x2)
            for k in range(PEER_TOPK):
                n = jnp.where(x1 >= cuts[k], k + 1.0, n)
                r = jnp.where(b[k] > x2, k + 1.0, r)
            cnt.append(n)
            rank.append(r)
            e1.append(jnp.exp(x1 - a[0]) * inv_z)
            e2.append(jnp.exp(x2 - b[0]))
        cnt_ref[h, grp] = jnp.concatenate(cnt, axis=0)
        e1_ref[h, grp] = jnp.concatenate(e1, axis=0)
        for r in range(ROW_BLOCKS):
            row = pl.multiple_of((r * PEER_HEADS + h) * PAIR_ROWS, PAIR_ROWS)
            pair = jnp.concatenate(rank[2 * r:2 * r + 2] + e2[2 * r:2 * r + 2], axis=0)
            sec_ref[grp, pl.ds(row, PAIR_ROWS), :] = pair

    def per_head_pair(i, carry):
        for slot in range(2):
            h = 2 * i + slot
            head_scores(jnp.minimum(h + 1, PEER_HEADS - 1), 1 - slot)
            for grp in range(groups):
                one(h, slot, grp)
        return carry

    head_scores(jnp.int32(0), 0)
    lax.fori_loop(0, PEER_HEADS // 2, per_head_pair, 0)


def _scores(xn, wq_t_bf, keys_bf, tb):
    t = xn.shape[1]
    groups = tb // LANES
    first = pl.BlockSpec((PEER_HEADS, groups, N_KEYS, LANES), lambda i: (0, i, 0, 0))
    return pl.pallas_call(
        functools.partial(_scores_kernel, tb=tb),
        grid=(t // tb,),
        in_specs=[pl.BlockSpec((D_MODEL, tb), lambda i: (0, i)),
                  pl.BlockSpec(wq_t_bf.shape, lambda i: (0, 0)),
                  pl.BlockSpec(keys_bf.shape, lambda i: (0, 0, 0))],
        out_specs=[first, first, pl.BlockSpec((groups, SECOND_ROWS, LANES), lambda i: (i, 0, 0))],
        out_shape=[jax.ShapeDtypeStruct((PEER_HEADS, t // LANES, N_KEYS, LANES), F32)] * 2
                  + [jax.ShapeDtypeStruct((t // LANES, SECOND_ROWS, LANES), F32)],
        scratch_shapes=[pltpu.VMEM((2, 2, N_KEYS, tb), F32)],
        compiler_params=_params("parallel"),
        name="peer_scores",
    )(xn, wq_t_bf, keys_bf)


CHUNK_TOKENS = 256
SUB_EXPERTS = SUBLANES * N_KEYS


def _experts_kernel(xn_ref, h_ref, u_ref, vt_ref, u_next_ref, vt_prev_ref, cnt_ref, e1_ref, sec_ref, y_ref,
                    hu_ref, w_ref, acc_ref, *, tb, eb):
    e = pl.program_id(1)
    last_step = pl.num_programs(1) - 1
    chunks = tb // CHUNK_TOKENS
    per_chunk = CHUNK_TOKENS // LANES
    stages = [(s, c) for s in range(eb // SUB_EXPERTS) for c in range(chunks)]
    n_stages = len(stages)

    def project(k, src_ref):
        s, c = stages[k]
        xc = xn_ref[:, c * CHUNK_TOKENS:(c + 1) * CHUNK_TOKENS]
        hu = jnp.dot(src_ref[s * SUB_EXPERTS:(s + 1) * SUB_EXPERTS, :], xc, preferred_element_type=F32)
        for j in range(per_chunk):
            hu_ref[k % 2, j] = hu[:, j * LANES:(j + 1) * LANES]

    @pl.when(e == 0)
    def _():
        acc_ref[...] = jnp.zeros_like(acc_ref)
        w_ref[(n_stages - 1) % 2] = jnp.zeros(w_ref.shape[1:], w_ref.dtype)
        project(0, u_ref)

    def weigh(k):
        s, c = stages[k]
        for j in range(per_chunk):
            grp = c * per_chunk + j
            for i1 in range(SUBLANES):
                first = s * SUBLANES + i1
                row = lambda ref, h: jnp.broadcast_to(ref[h, grp, first:first + 1, :], (SUBLANES, LANES))
                cnt = [row(cnt_ref, h) for h in range(PEER_HEADS)]
                e1 = [row(e1_ref, h) for h in range(PEER_HEADS)]
                for r in range(ROW_BLOCKS):
                    gate = [jnp.zeros((SUBLANES, LANES), F32) for _ in range(2)]
                    for h in range(PEER_HEADS):
                        base = (r * PEER_HEADS + h) * PAIR_ROWS
                        for half in range(2):
                            lo = base + half * SUBLANES
                            rank = sec_ref[grp, lo:lo + SUBLANES, :]
                            e2 = sec_ref[grp, lo + ROWS_BF16:lo + ROWS_BF16 + SUBLANES, :]
                            gate[half] = gate[half] + jnp.where(rank < cnt[h], e1[h] * e2, 0.0)
                    erow = slice(i1 * N_KEYS + r * ROWS_BF16, i1 * N_KEYS + (r + 1) * ROWS_BF16)
                    x = hu_ref[k % 2, j, erow, :]
                    act = x * (1.0 + lax.erf(x * np.float32(2.0 ** -0.5)))
                    w_ref[k % 2, erow, j * LANES:(j + 1) * LANES] = (act * jnp.concatenate(gate, axis=0)).astype(BF16)

    def collect(k, src_ref):
        s, c = stages[k]
        acc_ref[c] += jnp.dot(src_ref[:, s * SUB_EXPERTS:(s + 1) * SUB_EXPERTS], w_ref[k % 2],
                              preferred_element_type=F32)

    for k in range(n_stages):
        if k + 1 < n_stages:
            project(k + 1, u_ref)
        else:
            project(0, u_next_ref)
        if k > 0:
            collect(k - 1, vt_ref)
        else:
            collect(n_stages - 1, vt_prev_ref)
        weigh(k)

    @pl.when(e == last_step)
    def _():
        collect(n_stages - 1, vt_ref)
        for c in range(chunks):
            rows = slice(c * CHUNK_TOKENS, (c + 1) * CHUNK_TOKENS)
            y_ref[rows, :] = h_ref[rows, :] + acc_ref[c].T


def _experts(xn, h, u_bf, vt_bf, cnt, e1, sec, tb, eb):
    t = h.shape[0]
    groups = tb // LANES
    tok = lambda **kw: pl.BlockSpec((tb, D_MODEL), lambda i, j: (i, 0), **kw)
    once = dict(pipeline_mode=pl.Buffered(1))
    first = pl.BlockSpec((PEER_HEADS, groups, eb // N_KEYS, LANES), lambda i, j: (0, i, j, 0))
    steps = N_EXPERTS // eb
    assert eb == SUB_EXPERTS and (tb // CHUNK_TOKENS) % 2 == 0
    return pl.pallas_call(
        functools.partial(_experts_kernel, tb=tb, eb=eb),
        grid=(t // tb, steps),
        in_specs=[pl.BlockSpec((D_MODEL, tb), lambda i, j: (0, i)), tok(**once),
                  pl.BlockSpec((eb, D_MODEL), lambda i, j: (j, 0)),
                  pl.BlockSpec((D_MODEL, eb), lambda i, j: (0, j)),
                  pl.BlockSpec((eb, D_MODEL), lambda i, j: (jnp.minimum(j + 1, steps - 1), 0)),
                  pl.BlockSpec((D_MODEL, eb), lambda i, j: (0, jnp.maximum(j - 1, 0))),
                  first, first, pl.BlockSpec((groups, SECOND_ROWS, LANES), lambda i, j: (i, 0, 0), **once)],
        out_specs=tok(),
        out_shape=jax.ShapeDtypeStruct((t, D_MODEL), F32),
        scratch_shapes=[pltpu.VMEM((2, CHUNK_TOKENS // LANES, SUB_EXPERTS, LANES), F32),
                        pltpu.VMEM((2, SUB_EXPERTS, CHUNK_TOKENS), BF16),
                        pltpu.VMEM((tb // CHUNK_TOKENS, D_MODEL, CHUNK_TOKENS), F32)],
        compiler_params=_params("parallel", "arbitrary"),
        name="peer_experts",
    )(xn, h, u_bf, vt_bf, u_bf, vt_bf, cnt, e1, sec)


def _q_perm():
    n = np.arange(ATTN_DIM)
    g, j, d = n // LANES, (n % LANES) // HEAD_DIM, n % HEAD_DIM
    return j * (GQA_GROUP * HEAD_DIM) + g * HEAD_DIM + d


def _segment_matrix():
    i = np.arange(QK_DIM)
    return (i[:, None] // HEAD_DIM == i[None, :] // HEAD_DIM).astype(np.float32)


def _sink_rows(sinks, s_rows):
    s = sinks.astype(F32).reshape(N_KV_HEADS, GQA_GROUP, 1, 1)
    return jnp.broadcast_to(s, (N_KV_HEADS, GQA_GROUP, s_rows, 1)).reshape(N_KV_HEADS, GQA_GROUP * s_rows, 1)


def _layer(l, xp, xs, cache_k, cache_v, state_conv, p):
    (norm1_g, w_in, q_norm_g, k_norm_g, attn_sinks, conv_w, conv_b, conv_ln_g, conv_ln_b, conv_pw_w,
     attn_out_g, conv_out_g, w_out, norm2_g, peer_wq, peer_keys, peer_u, peer_v) = [a[l] for a in p]
    bsz, seq, _ = xp.shape
    dbs, dseq, _ = xs.shape
    perm = _q_perm()

    w_in_bf = jnp.concatenate([w_in[:, :ATTN_DIM][:, perm], w_in[:, ATTN_DIM:]], axis=1).astype(BF16)
    seg = jnp.asarray(_segment_matrix(), BF16)
    gqk = jnp.concatenate([jnp.tile(q_norm_g, N_KV_HEADS * GQA_GROUP) * (HEAD_DIM ** -0.5),
                           jnp.tile(k_norm_g, N_KV_HEADS)])[None, :]
    g1 = norm1_g[None, :]
    ga = attn_out_g[perm][None, :]
    gc = conv_out_g[None, :]
    wa_bf = w_out[:ATTN_DIM][perm].astype(BF16)
    wc_bf = w_out[ATTN_DIM:].astype(BF16)
    g2 = norm2_g[None, :]
    cw = jnp.pad(conv_w, ((0, 1), (0, 0)))
    cb, lg, lb = conv_b[None, :], conv_ln_g[None, :], conv_ln_b[None, :]
    pw_bf = conv_pw_w.astype(BF16)
    wq_t_bf = peer_wq.T.astype(BF16)
    keys_bf = peer_keys.reshape(2 * PEER_HEADS, N_KEYS, PEER_HALF).astype(BF16)
    u_bf = peer_u.astype(BF16)
    vt_bf = peer_v.T.astype(BF16)

    def peer(xn, h):
        cnt, e1, sec = _scores(xn, wq_t_bf, keys_bf, 512)
        return _experts(xn, h, u_bf, vt_bf, cnt, e1, sec, 1024, SUB_EXPERTS)

    tp = bsz * seq
    xpf = xp.reshape(tp, D_MODEL)
    q, k, v, glu = _proj(xpf, g1, w_in_bf, seg, gqk, 512)
    q3, k3, v3 = q.reshape(bsz, seq, ATTN_DIM), k.reshape(bsz, seq, KV_DIM), v.reshape(bsz, seq, KV_DIM)
    a = _attn_prompt(q3, k3, v3, _sink_rows(attn_sinks, WINDOW), 4).reshape(tp, ATTN_DIM)
    tile = 512
    glu4 = glu.reshape(bsz, seq // tile, tile, CONV_CH)
    hist = jnp.concatenate([jnp.zeros((bsz, 1, HIST_ROWS, CONV_CH), F32), glu4[:, :-1, -HIST_ROWS:]], axis=1)
    c = _conv(hist.reshape(-1, HIST_ROWS, CONV_CH), glu4.reshape(-1, tile, CONV_CH), cw, cb, lg, lb, pw_bf, 1, 32)
    h, xn = _merge(xpf, a, c, ga, gc, wa_bf, wc_bf, g2, 512)
    yp = peer(xn, h).reshape(bsz, seq, D_MODEL)
    pk = k3[:, -WINDOW:].reshape(bsz, WINDOW, N_KV_HEADS, HEAD_DIM)
    pv = v3[:, -WINDOW:].reshape(bsz, WINDOW, N_KV_HEADS, HEAD_DIM)
    pc = glu.reshape(bsz, seq, CONV_CH)[:, -(CONV_WIDTH - 1):]

    ts = dbs * dseq
    xsf = xs.reshape(ts, D_MODEL)
    q, k, v, glu = _proj(xsf, g1, w_in_bf, seg, gqk, 512)
    q3, k3, v3 = q.reshape(dbs, dseq, ATTN_DIM), k.reshape(dbs, dseq, KV_DIM), v.reshape(dbs, dseq, KV_DIM)
    ck = cache_k.reshape(dbs, -1, KV_DIM)
    cv = cache_v.reshape(dbs, -1, KV_DIM)
    a = _attn_sample(q3, ck, k3, cv, v3, _sink_rows(attn_sinks, dseq), 32).reshape(ts, ATTN_DIM)
    glu3 = glu.reshape(dbs, dseq, CONV_CH)
    hist = jnp.pad(state_conv, ((0, 0), (HIST_SKIP, 0), (0, 0)))
    c = _conv(hist, glu3, cw, cb, lg, lb, pw_bf, 64, dseq)
    h, xn = _merge(xsf, a, c, ga, gc, wa_bf, wc_bf, g2, 512)
    ys = peer(xn, h).reshape(dbs, dseq, D_MODEL)
    wb = ck.shape[1]
    sk = jnp.concatenate([ck, k3], axis=1)[:, -wb:].reshape(dbs, wb, N_KV_HEADS, HEAD_DIM)
    sv = jnp.concatenate([cv, v3], axis=1)[:, -wb:].reshape(dbs, wb, N_KV_HEADS, HEAD_DIM)
    sc = jnp.concatenate([state_conv, glu3], axis=1)[:, -(CONV_WIDTH - 1):]
    return yp, ys, (pk, pv, pc, sk, sv, sc)


def kernel(x_prompt, x_sample, cache_k, cache_v, state_conv, norm1_g, w_in, q_norm_g, k_norm_g, attn_sinks,
           conv_w, conv_b, conv_ln_g, conv_ln_b, conv_pw_w, attn_out_g, conv_out_g, w_out, norm2_g, peer_wq,
           peer_keys, peer_u, peer_v):
    params = (norm1_g, w_in, q_norm_g, k_norm_g, attn_sinks, conv_w, conv_b, conv_ln_g, conv_ln_b, conv_pw_w,
              attn_out_g, conv_out_g, w_out, norm2_g, peer_wq, peer_keys, peer_u, peer_v)
    depth = w_in.shape[0]
    xp, xs = x_prompt, x_sample
    states = []
    for l in range(depth):
        xp, xs, st = _layer(l, xp, xs, cache_k[l], cache_v[l], state_conv[l], params)
        states.append(st)
    stacked = [jnp.stack([st[i] for st in states]) for i in range(6)]
    return (xp, xs, *stacked)
```

```python
import functools

import numpy as np
import jax
import jax.numpy as jnp
from jax import lax
from jax.experimental import pallas as pl
from jax.experimental.pallas import tpu as pltpu

F32 = jnp.float32
BF16 = jnp.bfloat16

D_MODEL = 1024
ATTN_DIM = 512
CONV_CH = 512
HEAD_DIM = 64
N_KV_HEADS = 2
GQA_GROUP = 4
KV_DIM = N_KV_HEADS * HEAD_DIM
WINDOW = 128
CONV_WIDTH = 31
IN_DIM = ATTN_DIM + 2 * KV_DIM + 2 * CONV_CH
QK_DIM = ATTN_DIM + KV_DIM
PEER_HEADS = 8
N_KEYS = 128
N_EXPERTS = N_KEYS * N_KEYS
PEER_TOPK = 16
PEER_HALF = 128
EPS = 1e-6
NEG = -1e30

SUBLANES = 8
LANES = 128
HIST_ROWS = 32
HIST_SKIP = HIST_ROWS - (CONV_WIDTH - 1)

VMEM_LIMIT = 56 * 1024 * 1024

ROW_TILE_TOKENS = 512
SCORE_TOKENS = 512
EXPERT_TOKENS = 1024
ATTN_BLOCKS = 4
ATTN_SEQS = 32
CONV_ROWS = 32
CONV_SEQS = 64


def _params(*sem, **kw):
    return pltpu.CompilerParams(dimension_semantics=sem, vmem_limit_bytes=VMEM_LIMIT, **kw)


def _proj_kernel(x_ref, g1_ref, w_ref, seg_ref, gqk_ref, q_ref, k_ref, v_ref, glu_ref):
    x = x_ref[...]
    xn = x * lax.rsqrt(jnp.mean(x * x, axis=-1, keepdims=True) + EPS) * g1_ref[...]
    z = jnp.dot(xn.astype(BF16), w_ref[...], preferred_element_type=F32)
    qk = z[:, :QK_DIM]
    sq = qk * qk
    hi = sq.astype(BF16)
    lo = (sq - hi.astype(F32)).astype(BF16)
    seg = seg_ref[...]
    ssum = jnp.dot(hi, seg, preferred_element_type=F32) + jnp.dot(lo, seg, preferred_element_type=F32)
    qkn = qk * lax.rsqrt(ssum * (1.0 / HEAD_DIM) + EPS) * gqk_ref[...]
    q_ref[...] = qkn[:, :ATTN_DIM]
    k_ref[...] = qkn[:, ATTN_DIM:]
    v_ref[...] = z[:, QK_DIM:QK_DIM + KV_DIM]
    val = z[:, QK_DIM + KV_DIM:QK_DIM + KV_DIM + CONV_CH]
    gate = z[:, QK_DIM + KV_DIM + CONV_CH:]
    glu_ref[...] = val * jax.nn.sigmoid(gate)


def _proj(x, g1, w_in_bf, seg, gqk, tm):
    t = x.shape[0]
    row = lambda n: pl.BlockSpec((tm, n), lambda i: (i, 0))
    full = lambda a: pl.BlockSpec(a.shape, lambda i: (0,) * a.ndim)
    return pl.pallas_call(
        _proj_kernel,
        grid=(t // tm,),
        in_specs=[row(D_MODEL), full(g1), full(w_in_bf), full(seg), full(gqk)],
        out_specs=[row(ATTN_DIM), row(KV_DIM), row(KV_DIM), row(CONV_CH)],
        out_shape=[jax.ShapeDtypeStruct((t, n), F32) for n in (ATTN_DIM, KV_DIM, KV_DIM, CONV_CH)],
        compiler_params=_params("parallel"),
        name="proj",
    )(x, g1, w_in_bf, seg, gqk)


def _attend(q, kk, vv, sink_ref, first):
    s_rows = q.shape[0]
    n_keys = WINDOW + s_rows
    qs = jnp.concatenate([q[:, c * LANES:(c + 1) * LANES] for c in range(GQA_GROUP)], axis=0)
    lane = lax.broadcasted_iota(jnp.int32, (1, LANES), 1)
    low = lane < HEAD_DIM
    qi = lax.broadcasted_iota(jnp.int32, (s_rows, n_keys), 0)
    kr = lax.broadcasted_iota(jnp.int32, (s_rows, n_keys), 1)
    vis = (kr > qi) & (kr <= qi + WINDOW)
    if first is not None:
        vis = vis & ((kr >= WINDOW) | jnp.logical_not(first))
    vis = jnp.concatenate([vis] * GQA_GROUP, axis=0)
    outs = []
    for j in range(N_KV_HEADS):
        qm = jnp.where(low if j == 0 else jnp.logical_not(low), qs, 0.0).astype(BF16)
        s = lax.dot_general(qm, kk, (((1,), (1,)), ((), ())), preferred_element_type=F32)
        s = jnp.where(vis, s, NEG)
        sink = sink_ref[j]
        m = jnp.maximum(jnp.max(s, axis=-1, keepdims=True), sink)
        p = jnp.exp(s - m)
        denom = jnp.sum(p, axis=-1, keepdims=True) + jnp.exp(sink - m)
        o = jnp.dot(p.astype(BF16), vv, preferred_element_type=F32)
        outs.append(o / denom)
    o = jnp.where(low, outs[0], outs[1])
    return [o[c * s_rows:(c + 1) * s_rows] for c in range(GQA_GROUP)]


def _attn_prompt_kernel(q_ref, kp_ref, kc_ref, vp_ref, vc_ref, sink_ref, a_ref, *, blocks):
    first = pl.program_id(1) == 0
    kk = jnp.concatenate([kp_ref[...], kc_ref[...]], axis=0).astype(BF16)
    vv = jnp.concatenate([vp_ref[...], vc_ref[...]], axis=0).astype(BF16)
    for b in range(blocks):
        rows = slice(b * WINDOW, (b + 1) * WINDOW)
        keys = slice(b * WINDOW, (b + 2) * WINDOW)
        cols = _attend(q_ref[rows, :], kk[keys], vv[keys], sink_ref, first if b == 0 else None)
        for c in range(GQA_GROUP):
            a_ref[rows, c * LANES:(c + 1) * LANES] = cols[c]


def _attn_prompt(q, k, v, sink_rows, blocks):
    b, t, _ = q.shape
    rows = blocks * WINDOW
    cur = lambda n: pl.BlockSpec((None, rows, n), lambda i, j: (i, j, 0))
    prev = lambda n: pl.BlockSpec((None, WINDOW, n), lambda i, j: (i, jnp.maximum(j * blocks - 1, 0), 0))
    return pl.pallas_call(
        functools.partial(_attn_prompt_kernel, blocks=blocks),
        grid=(b, t // rows),
        in_specs=[cur(ATTN_DIM), prev(KV_DIM), cur(KV_DIM), prev(KV_DIM), cur(KV_DIM),
                  pl.BlockSpec(sink_rows.shape, lambda i, j: (0, 0, 0))],
        out_specs=cur(ATTN_DIM),
        out_shape=jax.ShapeDtypeStruct((b, t, ATTN_DIM), F32),
        compiler_params=_params("parallel", "parallel"),
        name="attn_prompt",
    )(q, k, k, v, v, sink_rows)


def _attn_sample_kernel(q_ref, kp_ref, kc_ref, vp_ref, vc_ref, sink_ref, a_ref, *, seqs):
    together = 4

    def one(n, carry):
        for u in range(together):
            i = n * together + u
            kk = jnp.concatenate([kp_ref[i], kc_ref[i]], axis=0).astype(BF16)
            vv = jnp.concatenate([vp_ref[i], vc_ref[i]], axis=0).astype(BF16)
            cols = _attend(q_ref[i], kk, vv, sink_ref, None)
            a_ref[i] = jnp.concatenate(cols, axis=1)
        return carry
    lax.fori_loop(0, seqs // together, one, 0)


def _attn_sample(q, ck, k, cv, v, sink_rows, seqs):
    b, s, _ = q.shape
    blk = lambda r, n: pl.BlockSpec((seqs, r, n), lambda i: (i, 0, 0))
    return pl.pallas_call(
        functools.partial(_attn_sample_kernel, seqs=seqs),
        grid=(b // seqs,),
        in_specs=[blk(s, ATTN_DIM), blk(WINDOW, KV_DIM), blk(s, KV_DIM), blk(WINDOW, KV_DIM), blk(s, KV_DIM),
                  pl.BlockSpec(sink_rows.shape, lambda i: (0, 0, 0))],
        out_specs=blk(s, ATTN_DIM),
        out_shape=jax.ShapeDtypeStruct((b, s, ATTN_DIM), F32),
        compiler_params=_params("parallel"),
        name="attn_sample",
    )(q, ck, k, cv, v, sink_rows)


def _conv_kernel(hist_ref, cur_ref, w_ref, b_ref, lg_ref, lb_ref, pw_ref, out_ref, sh_ref, acc_ref,
                 *, seqs, rows, chunk):
    span = rows + HIST_ROWS - SUBLANES

    def one(i, carry):
        sh_ref[0, 0:HIST_ROWS, :] = hist_ref[i]
        sh_ref[0, HIST_ROWS:HIST_ROWS + rows, :] = cur_ref[i]
        for s in range(1, SUBLANES):
            sh_ref[s, 0:span, :] = sh_ref[0, s:s + span, :]
        base = pl.multiple_of(i * rows, SUBLANES)
        for c in range(rows // chunk):
            acc = jnp.broadcast_to(b_ref[...], (chunk, CONV_CH))
            for j in range(CONV_WIDTH):
                off = HIST_SKIP + j
                lo = c * chunk + off - off % SUBLANES
                acc = acc + sh_ref[off % SUBLANES, lo:lo + chunk, :] * w_ref[j:j + 1, :]
            acc_ref[pl.ds(base + c * chunk, chunk), :] = acc
        return carry
    lax.fori_loop(0, seqs, one, 0)
    c = acc_ref[...]
    mu = jnp.mean(c, axis=-1, keepdims=True)
    d = c - mu
    var = jnp.mean(d * d, axis=-1, keepdims=True)
    y = d * lax.rsqrt(var + EPS) * lg_ref[...] + lb_ref[...]
    y = y * jax.nn.sigmoid(y)
    out_ref[...] = jnp.dot(y.astype(BF16), pw_ref[...], preferred_element_type=F32)


def _conv(hist, cur, w, b, lg, lb, pw_bf, seqs, chunk):
    n, rows, _ = cur.shape
    full = lambda a: pl.BlockSpec(a.shape, lambda i: (0,) * a.ndim)
    return pl.pallas_call(
        functools.partial(_conv_kernel, seqs=seqs, rows=rows, chunk=chunk),
        grid=(n // seqs,),
        in_specs=[pl.BlockSpec((seqs, HIST_ROWS, CONV_CH), lambda i: (i, 0, 0)),
                  pl.BlockSpec((seqs, rows, CONV_CH), lambda i: (i, 0, 0)),
                  full(w), full(b), full(lg), full(lb), full(pw_bf)],
        out_specs=pl.BlockSpec((seqs * rows, CONV_CH), lambda i: (i, 0)),
        out_shape=jax.ShapeDtypeStruct((n * rows, CONV_CH), F32),
        scratch_shapes=[pltpu.VMEM((SUBLANES, HIST_ROWS + rows, CONV_CH), F32),
                        pltpu.VMEM((seqs * rows, CONV_CH), F32)],
        compiler_params=_params("parallel"),
        name="conv",
    )(hist, cur, w, b, lg, lb, pw_bf)


def _rms(x, g):
    return x * lax.rsqrt(jnp.mean(x * x, axis=-1, keepdims=True) + EPS) * g


def _merge_kernel(x_ref, a_ref, c_ref, ga_ref, gc_ref, wa_ref, wc_ref, g2_ref, h_ref, xn_ref):
    an = _rms(a_ref[...], ga_ref[...]).astype(BF16)
    cn = _rms(c_ref[...], gc_ref[...]).astype(BF16)
    h = (x_ref[...] + jnp.dot(an, wa_ref[...], preferred_element_type=F32)
         + jnp.dot(cn, wc_ref[...], preferred_element_type=F32))
    h_ref[...] = h
    xn_ref[...] = _rms(h, g2_ref[...]).T.astype(BF16)


def _merge(x, a, c, ga, gc, wa_bf, wc_bf, g2, tm):
    t = x.shape[0]
    row = lambda n: pl.BlockSpec((tm, n), lambda i: (i, 0))
    full = lambda a_: pl.BlockSpec(a_.shape, lambda i: (0,) * a_.ndim)
    return pl.pallas_call(
        _merge_kernel,
        grid=(t // tm,),
        in_specs=[row(D_MODEL), row(ATTN_DIM), row(CONV_CH), full(ga), full(gc), full(wa_bf), full(wc_bf), full(g2)],
        out_specs=[row(D_MODEL), pl.BlockSpec((D_MODEL, tm), lambda i: (0, i))],
        out_shape=[jax.ShapeDtypeStruct((t, D_MODEL), F32), jax.ShapeDtypeStruct((D_MODEL, t), BF16)],
        compiler_params=_params("parallel"),
        name="merge",
    )(x, a, c, ga, gc, wa_bf, wc_bf, g2)


def _oddeven_merge_sort_pairs(n):
    pairs = []
    p = 1
    while p < n:
        k = p
        while k >= 1:
            for j in range(k % p, n - k, 2 * k):
                for i in range(min(k, n - j - k)):
                    if (i + j) // (2 * p) == (i + j + k) // (2 * p):
                        pairs.append((i + j, i + j + k))
            k //= 2
        p *= 2
    return pairs


_SORT16 = _oddeven_merge_sort_pairs(PEER_TOPK)


def _sort_desc(v):
    v = list(v)
    for i, j in _SORT16:
        v[i], v[j] = jnp.maximum(v[i], v[j]), jnp.minimum(v[i], v[j])
    return v


def _bitonic_to_desc(z):
    z = list(z)
    d = PEER_TOPK // 2
    while d >= 1:
        for i in range(PEER_TOPK):
            if i & d == 0:
                z[i], z[i + d] = jnp.maximum(z[i], z[i + d]), jnp.minimum(z[i], z[i + d])
        d //= 2
    return z


def _merge_top(x, y):
    return _bitonic_to_desc([jnp.maximum(x[i], y[PEER_TOPK - 1 - i]) for i in range(PEER_TOPK)])


def _across_sublanes(x):
    for shift in (4, 2, 1):
        x = _merge_top(x, [pltpu.roll(t, shift, 0) for t in x])
    return x


def _top16(tile):
    v = [tile[i * SUBLANES:(i + 1) * SUBLANES, :] for i in range(N_KEYS // SUBLANES)]
    return _across_sublanes(_sort_desc(v))


def _best_sums(a, b):
    sub = lax.broadcasted_iota(jnp.int32, (SUBLANES, LANES), 0)
    a_lo, a_hi = a[0], a[SUBLANES]
    for r in range(1, SUBLANES):
        a_lo = jnp.where(sub == r, a[r], a_lo)
        a_hi = jnp.where(sub == r, a[SUBLANES + r], a_hi)
    x = [jnp.where(sub < PEER_TOPK // (j + 1), a_lo + b[j], -jnp.inf) for j in range(PEER_TOPK)]
    x[PEER_TOPK - 1] = jnp.maximum(x[PEER_TOPK - 1], a_hi + b[0])
    return _across_sublanes(_bitonic_to_desc(x))


def _first_key_cuts(a, b, thr):
    cuts = []
    for j in range(PEER_TOPK):
        cut = jnp.full_like(thr, jnp.inf)
        for k in range(PEER_TOPK // (j + 1)):
            cut = jnp.where(a[k] + b[j] >= thr, a[k], cut)
        cuts.append(cut)
    return cuts


ROWS_BF16 = 2 * SUBLANES
ROW_BLOCKS = N_KEYS // ROWS_BF16
PAIR_ROWS = 2 * ROWS_BF16
SECOND_ROWS = ROW_BLOCKS * PEER_HEADS * PAIR_ROWS


def _scores_kernel(xn_ref, wq_ref, keys_ref, cnt_ref, e1_ref, sec_ref, s_ref, *, tb):
    groups = tb // LANES
    tiles = N_KEYS // SUBLANES

    def head_scores(h, slot):
        for p in range(2):
            rows = pl.ds(pl.multiple_of((2 * h + p) * PEER_HALF, PEER_HALF), PEER_HALF)
            qt = jnp.dot(wq_ref[rows, :], xn_ref[...], preferred_element_type=F32)
            s_ref[slot, p] = jnp.dot(keys_ref[2 * h + p], qt.astype(BF16), preferred_element_type=F32)

    def one(h, slot, grp):
        lanes = slice(grp * LANES, (grp + 1) * LANES)
        s1 = s_ref[slot, 0, :, lanes]
        s2 = s_ref[slot, 1, :, lanes]
        a = _top16(s1)
        b = _top16(s2)
        c = _best_sums(a, b)
        z = jnp.ones_like(c[0])
        for k in range(1, PEER_TOPK):
            z = z + jnp.exp(c[k] - c[0])
        inv_z = 0.5 / z
        cuts = _first_key_cuts(a, b, c[PEER_TOPK - 1])
        cnt, e1, rank, e2 = [], [], [], []
        for i in range(tiles):
            x1 = s1[i * SUBLANES:(i + 1) * SUBLANES, :]
            x2 = s2[i * SUBLANES:(i + 1) * SUBLANES, :]
            n = jnp.zeros_like(x1)
            r = jnp.zeros_like(x2)
            for k in range(PEER_TOPK):
                n = jnp.where(x1 >= cuts[k], k + 1.0, n)
                r = jnp.where(b[k] > x2, k + 1.0, r)
            cnt.append(n)
            rank.append(r)
            e1.append(jnp.exp(x1 - a[0]) * inv_z)
            e2.append(jnp.exp(x2 - b[0]))
        cnt_ref[h, grp] = jnp.concatenate(cnt, axis=0)
        e1_ref[h, grp] = jnp.concatenate(e1, axis=0)
        for r in range(ROW_BLOCKS):
            row = pl.multiple_of((r * PEER_HEADS + h) * PAIR_ROWS, PAIR_ROWS)
            pair = jnp.concatenate(rank[2 * r:2 * r + 2] + e2[2 * r:2 * r + 2], axis=0)
            sec_ref[grp, pl.ds(row, PAIR_ROWS), :] = pair

    def per_head_pair(i, carry):
        for slot in range(2):
            h = 2 * i + slot
            head_scores(jnp.minimum(h + 1, PEER_HEADS - 1), 1 - slot)
            for grp in range(groups):
                one(h, slot, grp)
        return carry

    head_scores(jnp.int32(0), 0)
    lax.fori_loop(0, PEER_HEADS // 2, per_head_pair, 0)


def _scores(xn, wq_t_bf, keys_bf, tb):
    t = xn.shape[1]
    groups = tb // LANES
    first = pl.BlockSpec((PEER_HEADS, groups, N_KEYS, LANES), lambda i: (0, i, 0, 0))
    return pl.pallas_call(
        functools.partial(_scores_kernel, tb=tb),
        grid=(t // tb,),
        in_specs=[pl.BlockSpec((D_MODEL, tb), lambda i: (0, i)),
                  pl.BlockSpec(wq_t_bf.shape, lambda i: (0, 0)),
                  pl.BlockSpec(keys_bf.shape, lambda i: (0, 0, 0))],
        out_specs=[first, first, pl.BlockSpec((groups, SECOND_ROWS, LANES), lambda i: (i, 0, 0))],
        out_shape=[jax.ShapeDtypeStruct((PEER_HEADS, t // LANES, N_KEYS, LANES), F32)] * 2
                  + [jax.ShapeDtypeStruct((t // LANES, SECOND_ROWS, LANES), F32)],
        scratch_shapes=[pltpu.VMEM((2, 2, N_KEYS, tb), F32)],
        compiler_params=_params("parallel"),
        name="peer_scores",
    )(xn, wq_t_bf, keys_bf)


CHUNK_TOKENS = 256
SUB_EXPERTS = SUBLANES * N_KEYS


def _experts_kernel(xn_ref, h_ref, u_ref, vt_ref, u_next_ref, vt_prev_ref, cnt_ref, e1_ref, sec_ref, y_ref,
                    hu_ref, w_ref, acc_ref, *, tb, eb):
    e = pl.program_id(1)
    last_step = pl.num_programs(1) - 1
    chunks = tb // CHUNK_TOKENS
    per_chunk = CHUNK_TOKENS // LANES
    stages = [(s, c) for s in range(eb // SUB_EXPERTS) for c in range(chunks)]
    n_stages = len(stages)

    def project(k, src_ref):
        s, c = stages[k]
        xc = xn_ref[:, c * CHUNK_TOKENS:(c + 1) * CHUNK_TOKENS]
        hu = jnp.dot(src_ref[s * SUB_EXPERTS:(s + 1) * SUB_EXPERTS, :], xc, preferred_element_type=F32)
        for j in range(per_chunk):
            hu_ref[k % 2, j] = hu[:, j * LANES:(j + 1) * LANES]

    @pl.when(e == 0)
    def _():
        acc_ref[...] = jnp.zeros_like(acc_ref)
        w_ref[(n_stages - 1) % 2] = jnp.zeros(w_ref.shape[1:], w_ref.dtype)
        project(0, u_ref)

    def weigh(k):
        s, c = stages[k]
        for j in range(per_chunk):
            grp = c * per_chunk + j
            for i1 in range(SUBLANES):
                first = s * SUBLANES + i1
                row = lambda ref, h: jnp.broadcast_to(ref[h, grp, first:first + 1, :], (SUBLANES, LANES))
                cnt = [row(cnt_ref, h) for h in range(PEER_HEADS)]
                e1 = [row(e1_ref, h) for h in range(PEER_HEADS)]
                for r in range(ROW_BLOCKS):
                    gate = [jnp.zeros((SUBLANES, LANES), F32) for _ in range(2)]
                    for h in range(PEER_HEADS):
                        base = (r * PEER_HEADS + h) * PAIR_ROWS
                        for half in range(2):
                            lo = base + half * SUBLANES
                            rank = sec_ref[grp, lo:lo + SUBLANES, :]
                            e2 = sec_ref[grp, lo + ROWS_BF16:lo + ROWS_BF16 + SUBLANES, :]
                            gate[half] = gate[half] + jnp.where(rank < cnt[h], e1[h] * e2, 0.0)
                    erow = slice(i1 * N_KEYS + r * ROWS_BF16, i1 * N_KEYS + (r + 1) * ROWS_BF16)
                    x = hu_ref[k % 2, j, erow, :]
                    act = x * (1.0 + lax.erf(x * np.float32(2.0 ** -0.5)))
                    w_ref[k % 2, erow, j * LANES:(j + 1) * LANES] = (act * jnp.concatenate(gate, axis=0)).astype(BF16)

    def collect(k, src_ref):
        s, c = stages[k]
        acc_ref[c] += jnp.dot(src_ref[:, s * SUB_EXPERTS:(s + 1) * SUB_EXPERTS], w_ref[k % 2],
                              preferred_element_type=F32)

    for k in range(n_stages):
        if k + 1 < n_stages:
            project(k + 1, u_ref)
        else:
            project(0, u_next_ref)
        if k > 0:
            collect(k - 1, vt_ref)
        else:
            collect(n_stages - 1, vt_prev_ref)
        weigh(k)

    @pl.when(e == last_step)
    def _():
        collect(n_stages - 1, vt_ref)
        for c in range(chunks):
            rows = slice(c * CHUNK_TOKENS, (c + 1) * CHUNK_TOKENS)
            y_ref[rows, :] = h_ref[rows, :] + acc_ref[c].T


def _experts(xn, h, u_bf, vt_bf, cnt, e1, sec, tb, eb):
    t = h.shape[0]
    groups = tb // LANES
    tok = lambda **kw: pl.BlockSpec((tb, D_MODEL), lambda i, j: (i, 0), **kw)
    once = dict(pipeline_mode=pl.Buffered(1))
    first = pl.BlockSpec((PEER_HEADS, groups, eb // N_KEYS, LANES), lambda i, j: (0, i, j, 0))
    steps = N_EXPERTS // eb
    assert eb == SUB_EXPERTS and (tb // CHUNK_TOKENS) % 2 == 0
    return pl.pallas_call(
        functools.partial(_experts_kernel, tb=tb, eb=eb),
        grid=(t // tb, steps),
        in_specs=[pl.BlockSpec((D_MODEL, tb), lambda i, j: (0, i)), tok(**once),
                  pl.BlockSpec((eb, D_MODEL), lambda i, j: (j, 0)),
                  pl.BlockSpec((D_MODEL, eb), lambda i, j: (0, j)),
                  pl.BlockSpec((eb, D_MODEL), lambda i, j: (jnp.minimum(j + 1, steps - 1), 0)),
                  pl.BlockSpec((D_MODEL, eb), lambda i, j: (0, jnp.maximum(j - 1, 0))),
                  first, first, pl.BlockSpec((groups, SECOND_ROWS, LANES), lambda i, j: (i, 0, 0), **once)],
        out_specs=tok(),
        out_shape=jax.ShapeDtypeStruct((t, D_MODEL), F32),
        scratch_shapes=[pltpu.VMEM((2, CHUNK_TOKENS // LANES, SUB_EXPERTS, LANES), F32),
                        pltpu.VMEM((2, SUB_EXPERTS, CHUNK_TOKENS), BF16),
                        pltpu.VMEM((tb // CHUNK_TOKENS, D_MODEL, CHUNK_TOKENS), F32)],
        compiler_params=_params("parallel", "arbitrary"),
        name="peer_experts",
    )(xn, h, u_bf, vt_bf, u_bf, vt_bf, cnt, e1, sec)


def _q_perm():
    n = np.arange(ATTN_DIM)
    g, j, d = n // LANES, (n % LANES) // HEAD_DIM, n % HEAD_DIM
    return j * (GQA_GROUP * HEAD_DIM) + g * HEAD_DIM + d


def _segment_matrix():
    i = np.arange(QK_DIM)
    return (i[:, None] // HEAD_DIM == i[None, :] // HEAD_DIM).astype(np.float32)


def _sink_rows(sinks, s_rows):
    s = sinks.astype(F32).reshape(N_KV_HEADS, GQA_GROUP, 1, 1)
    return jnp.broadcast_to(s, (N_KV_HEADS, GQA_GROUP, s_rows, 1)).reshape(N_KV_HEADS, GQA_GROUP * s_rows, 1)


def _layer(l, xp, xs, cache_k, cache_v, state_conv, p):
    (norm1_g, w_in, q_norm_g, k_norm_g, attn_sinks, conv_w, conv_b, conv_ln_g, conv_ln_b, conv_pw_w,
     attn_out_g, conv_out_g, w_out, norm2_g, peer_wq, peer_keys, peer_u, peer_v) = [a[l] for a in p]
    bsz, seq, _ = xp.shape
    dbs, dseq, _ = xs.shape
    perm = _q_perm()

    w_in_bf = jnp.concatenate([w_in[:, :ATTN_DIM][:, perm], w_in[:, ATTN_DIM:]], axis=1).astype(BF16)
    seg = jnp.asarray(_segment_matrix(), BF16)
    gqk = jnp.concatenate([jnp.tile(q_norm_g, N_KV_HEADS * GQA_GROUP) * (HEAD_DIM ** -0.5),
                           jnp.tile(k_norm_g, N_KV_HEADS)])[None, :]
    g1 = norm1_g[None, :]
    ga = attn_out_g[perm][None, :]
    gc = conv_out_g[None, :]
    wa_bf = w_out[:ATTN_DIM][perm].astype(BF16)
    wc_bf = w_out[ATTN_DIM:].astype(BF16)
    g2 = norm2_g[None, :]
    cw = jnp.pad(conv_w, ((0, 1), (0, 0)))
    cb, lg, lb = conv_b[None, :], conv_ln_g[None, :], conv_ln_b[None, :]
    pw_bf = conv_pw_w.astype(BF16)
    wq_t_bf = peer_wq.T.astype(BF16)
    keys_bf = peer_keys.reshape(2 * PEER_HEADS, N_KEYS, PEER_HALF).astype(BF16)
    u_bf = peer_u.astype(BF16)
    vt_bf = peer_v.astype(BF16).T

    def peer(xn, h):
        cnt, e1, sec = _scores(xn, wq_t_bf, keys_bf, SCORE_TOKENS)
        return _experts(xn, h, u_bf, vt_bf, cnt, e1, sec, EXPERT_TOKENS, SUB_EXPERTS)

    tp = bsz * seq
    xpf = xp.reshape(tp, D_MODEL)
    q, k, v, glu = _proj(xpf, g1, w_in_bf, seg, gqk, ROW_TILE_TOKENS)
    q3, k3, v3 = q.reshape(bsz, seq, ATTN_DIM), k.reshape(bsz, seq, KV_DIM), v.reshape(bsz, seq, KV_DIM)
    a = _attn_prompt(q3, k3, v3, _sink_rows(attn_sinks, WINDOW), ATTN_BLOCKS).reshape(tp, ATTN_DIM)
    tile = ROW_TILE_TOKENS
    glu4 = glu.reshape(bsz, seq // tile, tile, CONV_CH)
    hist = jnp.concatenate([jnp.zeros((bsz, 1, HIST_ROWS, CONV_CH), F32), glu4[:, :-1, -HIST_ROWS:]], axis=1)
    c = _conv(hist.reshape(-1, HIST_ROWS, CONV_CH), glu4.reshape(-1, tile, CONV_CH), cw, cb, lg, lb, pw_bf, 1, CONV_ROWS)
    h, xn = _merge(xpf, a, c, ga, gc, wa_bf, wc_bf, g2, ROW_TILE_TOKENS)
    yp = peer(xn, h).reshape(bsz, seq, D_MODEL)
    pk = k3[:, -WINDOW:].reshape(bsz, WINDOW, N_KV_HEADS, HEAD_DIM)
    pv = v3[:, -WINDOW:].reshape(bsz, WINDOW, N_KV_HEADS, HEAD_DIM)
    pc = glu.reshape(bsz, seq, CONV_CH)[:, -(CONV_WIDTH - 1):]

    ts = dbs * dseq
    xsf = xs.reshape(ts, D_MODEL)
    q, k, v, glu = _proj(xsf, g1, w_in_bf, seg, gqk, ROW_TILE_TOKENS)
    q3, k3, v3 = q.reshape(dbs, dseq, ATTN_DIM), k.reshape(dbs, dseq, KV_DIM), v.reshape(dbs, dseq, KV_DIM)
    ck = cache_k.reshape(dbs, -1, KV_DIM)
    cv = cache_v.reshape(dbs, -1, KV_DIM)
    a = _attn_sample(q3, ck, k3, cv, v3, _sink_rows(attn_sinks, dseq), ATTN_SEQS).reshape(ts, ATTN_DIM)
    glu3 = glu.reshape(dbs, dseq, CONV_CH)
    hist = jnp.pad(state_conv, ((0, 0), (HIST_SKIP, 0), (0, 0)))
    c = _conv(hist, glu3, cw, cb, lg, lb, pw_bf, CONV_SEQS, dseq)
    h, xn = _merge(xsf, a, c, ga, gc, wa_bf, wc_bf, g2, ROW_TILE_TOKENS)
    ys = peer(xn, h).reshape(dbs, dseq, D_MODEL)
    wb = ck.shape[1]
    sk = jnp.concatenate([ck, k3], axis=1)[:, -wb:].reshape(dbs, wb, N_KV_HEADS, HEAD_DIM)
    sv = jnp.concatenate([cv, v3], axis=1)[:, -wb:].reshape(dbs, wb, N_KV_HEADS, HEAD_DIM)
    sc = jnp.concatenate([state_conv, glu3], axis=1)[:, -(CONV_WIDTH - 1):]
    return yp, ys, (pk, pv, pc, sk, sv, sc)


def kernel(x_prompt, x_sample, cache_k, cache_v, state_conv, norm1_g, w_in, q_norm_g, k_norm_g, attn_sinks,
           conv_w, conv_b, conv_ln_g, conv_ln_b, conv_pw_w, attn_out_g, conv_out_g, w_out, norm2_g, peer_wq,
           peer_keys, peer_u, peer_v):
    params = (norm1_g, w_in, q_norm_g, k_norm_g, attn_sinks, conv_w, conv_b, conv_ln_g, conv_ln_b, conv_pw_w,
              attn_out_g, conv_out_g, w_out, norm2_g, peer_wq, peer_keys, peer_u, peer_v)
    depth = w_in.shape[0]
    xp, xs = x_prompt, x_sample
    states = []
    for l in range(depth):
        xp, xs, st = _layer(l, xp, xs, cache_k[l], cache_v[l], state_conv[l], params)
        states.append(st)
    stacked = [jnp.stack([st[i] for st in states]) for i in range(6)]
    return (xp, xs, *stacked)
```

```python
import functools

import numpy as np
import jax
import jax.numpy as jnp
from jax import lax
from jax.experimental import pallas as pl
from jax.experimental.pallas import tpu as pltpu

F32 = jnp.float32
BF16 = jnp.bfloat16

D_MODEL = 1024
ATTN_DIM = 512
CONV_CH = 512
HEAD_DIM = 64
N_KV_HEADS = 2
GQA_GROUP = 4
KV_DIM = N_KV_HEADS * HEAD_DIM
WINDOW = 128
CONV_WIDTH = 31
IN_DIM = ATTN_DIM + 2 * KV_DIM + 2 * CONV_CH
QK_DIM = ATTN_DIM + KV_DIM
PEER_HEADS = 8
N_KEYS = 128
N_EXPERTS = N_KEYS * N_KEYS
PEER_TOPK = 16
PEER_HALF = 128
EPS = 1e-6
NEG = -1e30

SUBLANES = 8
LANES = 128
HIST_ROWS = 32
HIST_SKIP = HIST_ROWS - (CONV_WIDTH - 1)

VMEM_LIMIT = 56 * 1024 * 1024

ROW_TILE_TOKENS = 512
SCORE_TOKENS = 512
EXPERT_TOKENS = 1024
ATTN_BLOCKS = 8
ATTN_SEQS = 32
CONV_ROWS = 32
CONV_SEQS = 64


def _params(*sem, **kw):
    return pltpu.CompilerParams(dimension_semantics=sem, vmem_limit_bytes=VMEM_LIMIT, **kw)


def _proj_kernel(x_ref, g1_ref, w_ref, seg_ref, gqk_ref, q_ref, k_ref, v_ref, glu_ref):
    x = x_ref[...]
    xn = x * lax.rsqrt(jnp.mean(x * x, axis=-1, keepdims=True) + EPS) * g1_ref[...]
    z = jnp.dot(xn.astype(BF16), w_ref[...], preferred_element_type=F32)
    qk = z[:, :QK_DIM]
    sq = qk * qk
    hi = sq.astype(BF16)
    lo = (sq - hi.astype(F32)).astype(BF16)
    seg = seg_ref[...]
    ssum = jnp.dot(hi, seg, preferred_element_type=F32) + jnp.dot(lo, seg, preferred_element_type=F32)
    qkn = qk * lax.rsqrt(ssum * (1.0 / HEAD_DIM) + EPS) * gqk_ref[...]
    q_ref[...] = qkn[:, :ATTN_DIM]
    k_ref[...] = qkn[:, ATTN_DIM:]
    v_ref[...] = z[:, QK_DIM:QK_DIM + KV_DIM]
    val = z[:, QK_DIM + KV_DIM:QK_DIM + KV_DIM + CONV_CH]
    gate = z[:, QK_DIM + KV_DIM + CONV_CH:]
    glu_ref[...] = val * jax.nn.sigmoid(gate)


def _proj(x, g1, w_in_bf, seg, gqk, tm):
    t = x.shape[0]
    row = lambda n: pl.BlockSpec((tm, n), lambda i: (i, 0))
    full = lambda a: pl.BlockSpec(a.shape, lambda i: (0,) * a.ndim)
    return pl.pallas_call(
        _proj_kernel,
        grid=(t // tm,),
        in_specs=[row(D_MODEL), full(g1), full(w_in_bf), full(seg), full(gqk)],
        out_specs=[row(ATTN_DIM), row(KV_DIM), row(KV_DIM), row(CONV_CH)],
        out_shape=[jax.ShapeDtypeStruct((t, n), F32) for n in (ATTN_DIM, KV_DIM, KV_DIM, CONV_CH)],
        compiler_params=_params("parallel"),
        name="proj",
    )(x, g1, w_in_bf, seg, gqk)


def _attend(q, kk, vv, sink_ref, first):
    s_rows = q.shape[0]
    n_keys = WINDOW + s_rows
    qs = jnp.concatenate([q[:, c * LANES:(c + 1) * LANES] for c in range(GQA_GROUP)], axis=0)
    lane = lax.broadcasted_iota(jnp.int32, (1, LANES), 1)
    low = lane < HEAD_DIM
    qi = lax.broadcasted_iota(jnp.int32, (s_rows, n_keys), 0)
    kr = lax.broadcasted_iota(jnp.int32, (s_rows, n_keys), 1)
    vis = (kr > qi) & (kr <= qi + WINDOW)
    if first is not None:
        vis = vis & ((kr >= WINDOW) | jnp.logical_not(first))
    vis = jnp.concatenate([vis] * GQA_GROUP, axis=0)
    outs = []
    for j in range(N_KV_HEADS):
        qm = jnp.where(low if j == 0 else jnp.logical_not(low), qs, 0.0).astype(BF16)
        s = lax.dot_general(qm, kk, (((1,), (1,)), ((), ())), preferred_element_type=F32)
        s = jnp.where(vis, s, NEG)
        sink = sink_ref[j]
        m = jnp.maximum(jnp.max(s, axis=-1, keepdims=True), sink)
        p = jnp.exp(s - m)
        denom = jnp.sum(p, axis=-1, keepdims=True) + jnp.exp(sink - m)
        o = jnp.dot(p.astype(BF16), vv, preferred_element_type=F32)
        outs.append(o / denom)
    o = jnp.where(low, outs[0], outs[1])
    return [o[c * s_rows:(c + 1) * s_rows] for c in range(GQA_GROUP)]


def _attn_prompt_kernel(q_ref, kp_ref, kc_ref, vp_ref, vc_ref, sink_ref, a_ref, *, blocks):
    first = pl.program_id(1) == 0
    kk = jnp.concatenate([kp_ref[...], kc_ref[...]], axis=0).astype(BF16)
    vv = jnp.concatenate([vp_ref[...], vc_ref[...]], axis=0).astype(BF16)
    for b in range(blocks):
        rows = slice(b * WINDOW, (b + 1) * WINDOW)
        keys = slice(b * WINDOW, (b + 2) * WINDOW)
        cols = _attend(q_ref[rows, :], kk[keys], vv[keys], sink_ref, first if b == 0 else None)
        for c in range(GQA_GROUP):
            a_ref[rows, c * LANES:(c + 1) * LANES] = cols[c]


def _attn_prompt(q, k, v, sink_rows, blocks):
    b, t, _ = q.shape
    rows = blocks * WINDOW
    cur = lambda n: pl.BlockSpec((None, rows, n), lambda i, j: (i, j, 0))
    prev = lambda n: pl.BlockSpec((None, WINDOW, n), lambda i, j: (i, jnp.maximum(j * blocks - 1, 0), 0))
    return pl.pallas_call(
        functools.partial(_attn_prompt_kernel, blocks=blocks),
        grid=(b, t // rows),
        in_specs=[cur(ATTN_DIM), prev(KV_DIM), cur(KV_DIM), prev(KV_DIM), cur(KV_DIM),
                  pl.BlockSpec(sink_rows.shape, lambda i, j: (0, 0, 0))],
        out_specs=cur(ATTN_DIM),
        out_shape=jax.ShapeDtypeStruct((b, t, ATTN_DIM), F32),
        compiler_params=_params("parallel", "parallel"),
        name="attn_prompt",
    )(q, k, k, v, v, sink_rows)


def _attn_sample_kernel(q_ref, kp_ref, kc_ref, vp_ref, vc_ref, sink_ref, a_ref, *, seqs):
    together = 4

    def one(n, carry):
        for u in range(together):
            i = n * together + u
            kk = jnp.concatenate([kp_ref[i], kc_ref[i]], axis=0).astype(BF16)
            vv = jnp.concatenate([vp_ref[i], vc_ref[i]], axis=0).astype(BF16)
            cols = _attend(q_ref[i], kk, vv, sink_ref, None)
            a_ref[i] = jnp.concatenate(cols, axis=1)
        return carry
    lax.fori_loop(0, seqs // together, one, 0)


def _attn_sample(q, ck, k, cv, v, sink_rows, seqs):
    b, s, _ = q.shape
    blk = lambda r, n: pl.BlockSpec((seqs, r, n), lambda i: (i, 0, 0))
    return pl.pallas_call(
        functools.partial(_attn_sample_kernel, seqs=seqs),
        grid=(b // seqs,),
        in_specs=[blk(s, ATTN_DIM), blk(WINDOW, KV_DIM), blk(s, KV_DIM), blk(WINDOW, KV_DIM), blk(s, KV_DIM),
                  pl.BlockSpec(sink_rows.shape, lambda i: (0, 0, 0))],
        out_specs=blk(s, ATTN_DIM),
        out_shape=jax.ShapeDtypeStruct((b, s, ATTN_DIM), F32),
        compiler_params=_params("parallel"),
        name="attn_sample",
    )(q, ck, k, cv, v, sink_rows)


def _conv_kernel(hist_ref, cur_ref, w_ref, b_ref, lg_ref, lb_ref, pw_ref, out_ref, sh_ref, acc_ref,
                 *, seqs, rows, chunk):
    span = rows + HIST_ROWS - SUBLANES

    def one(i, carry):
        sh_ref[0, 0:HIST_ROWS, :] = hist_ref[i]
        sh_ref[0, HIST_ROWS:HIST_ROWS + rows, :] = cur_ref[i]
        for s in range(1, SUBLANES):
            sh_ref[s, 0:span, :] = sh_ref[0, s:s + span, :]
        base = pl.multiple_of(i * rows, SUBLANES)
        for c in range(rows // chunk):
            acc = jnp.broadcast_to(b_ref[...], (chunk, CONV_CH))
            for j in range(CONV_WIDTH):
                off = HIST_SKIP + j
                lo = c * chunk + off - off % SUBLANES
                acc = acc + sh_ref[off % SUBLANES, lo:lo + chunk, :] * w_ref[j:j + 1, :]
            acc_ref[pl.ds(base + c * chunk, chunk), :] = acc
        return carry
    lax.fori_loop(0, seqs, one, 0)
    c = acc_ref[...]
    mu = jnp.mean(c, axis=-1, keepdims=True)
    d = c - mu
    var = jnp.mean(d * d, axis=-1, keepdims=True)
    y = d * lax.rsqrt(var + EPS) * lg_ref[...] + lb_ref[...]
    y = y * jax.nn.sigmoid(y)
    out_ref[...] = jnp.dot(y.astype(BF16), pw_ref[...], preferred_element_type=F32)


def _conv(hist, cur, w, b, lg, lb, pw_bf, seqs, chunk):
    n, rows, _ = cur.shape
    full = lambda a: pl.BlockSpec(a.shape, lambda i: (0,) * a.ndim)
    return pl.pallas_call(
        functools.partial(_conv_kernel, seqs=seqs, rows=rows, chunk=chunk),
        grid=(n // seqs,),
        in_specs=[pl.BlockSpec((seqs, HIST_ROWS, CONV_CH), lambda i: (i, 0, 0)),
                  pl.BlockSpec((seqs, rows, CONV_CH), lambda i: (i, 0, 0)),
                  full(w), full(b), full(lg), full(lb), full(pw_bf)],
        out_specs=pl.BlockSpec((seqs * rows, CONV_CH), lambda i: (i, 0)),
        out_shape=jax.ShapeDtypeStruct((n * rows, CONV_CH), F32),
        scratch_shapes=[pltpu.VMEM((SUBLANES, HIST_ROWS + rows, CONV_CH), F32),
                        pltpu.VMEM((seqs * rows, CONV_CH), F32)],
        compiler_params=_params("parallel"),
        name="conv",
    )(hist, cur, w, b, lg, lb, pw_bf)


def _rms(x, g):
    return x * lax.rsqrt(jnp.mean(x * x, axis=-1, keepdims=True) + EPS) * g


def _merge_kernel(x_ref, a_ref, c_ref, ga_ref, gc_ref, wa_ref, wc_ref, g2_ref, h_ref, xn_ref):
    an = _rms(a_ref[...], ga_ref[...]).astype(BF16)
    cn = _rms(c_ref[...], gc_ref[...]).astype(BF16)
    h = (x_ref[...] + jnp.dot(an, wa_ref[...], preferred_element_type=F32)
         + jnp.dot(cn, wc_ref[...], preferred_element_type=F32))
    h_ref[...] = h
    xn_ref[...] = _rms(h, g2_ref[...]).T.astype(BF16)


def _merge(x, a, c, ga, gc, wa_bf, wc_bf, g2, tm):
    t = x.shape[0]
    row = lambda n: pl.BlockSpec((tm, n), lambda i: (i, 0))
    full = lambda a_: pl.BlockSpec(a_.shape, lambda i: (0,) * a_.ndim)
    return pl.pallas_call(
        _merge_kernel,
        grid=(t // tm,),
        in_specs=[row(D_MODEL), row(ATTN_DIM), row(CONV_CH), full(ga), full(gc), full(wa_bf), full(wc_bf), full(g2)],
        out_specs=[row(D_MODEL), pl.BlockSpec((D_MODEL, tm), lambda i: (0, i))],
        out_shape=[jax.ShapeDtypeStruct((t, D_MODEL), F32), jax.ShapeDtypeStruct((D_MODEL, t), BF16)],
        compiler_params=_params("parallel"),
        name="merge",
    )(x, a, c, ga, gc, wa_bf, wc_bf, g2)


def _oddeven_merge_sort_pairs(n):
    pairs = []
    p = 1
    while p < n:
        k = p
        while k >= 1:
            for j in range(k % p, n - k, 2 * k):
                for i in range(min(k, n - j - k)):
                    if (i + j) // (2 * p) == (i + j + k) // (2 * p):
                        pairs.append((i + j, i + j + k))
            k //= 2
        p *= 2
    return pairs


_SORT16 = _oddeven_merge_sort_pairs(PEER_TOPK)


def _sort_desc(v):
    v = list(v)
    for i, j in _SORT16:
        v[i], v[j] = jnp.maximum(v[i], v[j]), jnp.minimum(v[i], v[j])
    return v


def _bitonic_to_desc(z):
    z = list(z)
    d = PEER_TOPK // 2
    while d >= 1:
        for i in range(PEER_TOPK):
            if i & d == 0:
                z[i], z[i + d] = jnp.maximum(z[i], z[i + d]), jnp.minimum(z[i], z[i + d])
        d //= 2
    return z


def _merge_top(x, y):
    return _bitonic_to_desc([jnp.maximum(x[i], y[PEER_TOPK - 1 - i]) for i in range(PEER_TOPK)])


def _across_sublanes(x):
    for shift in (4, 2, 1):
        x = _merge_top(x, [pltpu.roll(t, shift, 0) for t in x])
    return x


def _top16(tile):
    v = [tile[i * SUBLANES:(i + 1) * SUBLANES, :] for i in range(N_KEYS // SUBLANES)]
    return _across_sublanes(_sort_desc(v))


def _best_sums(a, b):
    sub = lax.broadcasted_iota(jnp.int32, (SUBLANES, LANES), 0)
    a_lo, a_hi = a[0], a[SUBLANES]
    for r in range(1, SUBLANES):
        a_lo = jnp.where(sub == r, a[r], a_lo)
        a_hi = jnp.where(sub == r, a[SUBLANES + r], a_hi)
    x = [jnp.where(sub < PEER_TOPK // (j + 1), a_lo + b[j], -jnp.inf) for j in range(PEER_TOPK)]
    x[PEER_TOPK - 1] = jnp.maximum(x[PEER_TOPK - 1], a_hi + b[0])
    return _across_sublanes(_bitonic_to_desc(x))


def _first_key_cuts(a, b, thr):
    cuts = []
    for j in range(PEER_TOPK):
        cut = jnp.full_like(thr, jnp.inf)
        for k in range(PEER_TOPK // (j + 1)):
            cut = jnp.where(a[k] + b[j] >= thr, a[k], cut)
        cuts.append(cut)
    return cuts


ROWS_BF16 = 2 * SUBLANES
ROW_BLOCKS = N_KEYS // ROWS_BF16
PAIR_ROWS = 2 * ROWS_BF16
SECOND_ROWS = ROW_BLOCKS * PEER_HEADS * PAIR_ROWS


def _scores_kernel(xn_ref, wq_ref, keys_ref, cnt_ref, e1_ref, sec_ref, s_ref, *, tb):
    groups = tb // LANES
    tiles = N_KEYS // SUBLANES

    def head_scores(h, slot):
        for p in range(2):
            rows = pl.ds(pl.multiple_of((2 * h + p) * PEER_HALF, PEER_HALF), PEER_HALF)
            qt = jnp.dot(wq_ref[rows, :], xn_ref[...], preferred_element_type=F32)
            s_ref[slot, p] = jnp.dot(keys_ref[2 * h + p], qt.astype(BF16), preferred_element_type=F32)

    def one(h, slot, grp):
        lanes = slice(grp * LANES, (grp + 1) * LANES)
        s1 = s_ref[slot, 0, :, lanes]
        s2 = s_ref[slot, 1, :, lanes]
        a = _top16(s1)
        b = _top16(s2)
        c = _best_sums(a, b)
        z = jnp.ones_like(c[0])
        for k in range(1, PEER_TOPK):
            z = z + jnp.exp(c[k] - c[0])
        inv_z = 0.5 / z
        cuts = _first_key_cuts(a, b, c[PEER_TOPK - 1])
        cnt, e1, rank, e2 = [], [], [], []
        for i in range(tiles):
            x1 = s1[i * SUBLANES:(i + 1) * SUBLANES, :]
            x2 = s2[i * SUBLANES:(i + 1) * SUBLANES, :]
            n = jnp.zeros_like(x1)
            r = jnp.zeros_like(x2)
            for k in range(PEER_TOPK):
                n = jnp.where(x1 >= cuts[k], k + 1.0, n)
                r = jnp.where(b[k] > x2, k + 1.0, r)
            cnt.append(n)
            rank.append(r)
            e1.append(jnp.exp(x1 - a[0]) * inv_z)
            e2.append(jnp.exp(x2 - b[0]))
        cnt_ref[h, grp] = jnp.concatenate(cnt, axis=0)
        e1_ref[h, grp] = jnp.concatenate(e1, axis=0)
        for r in range(ROW_BLOCKS):
            row = pl.multiple_of((r * PEER_HEADS + h) * PAIR_ROWS, PAIR_ROWS)
            pair = jnp.concatenate(rank[2 * r:2 * r + 2] + e2[2 * r:2 * r + 2], axis=0)
            sec_ref[grp, pl.ds(row, PAIR_ROWS), :] = pair

    def per_head_pair(i, carry):
        for slot in range(2):
            h = 2 * i + slot
            head_scores(jnp.minimum(h + 1, PEER_HEADS - 1), 1 - slot)
            for grp in range(groups):
                one(h, slot, grp)
        return carry

    head_scores(jnp.int32(0), 0)
    lax.fori_loop(0, PEER_HEADS // 2, per_head_pair, 0)


def _scores(xn, wq_t_bf, keys_bf, tb):
    t = xn.shape[1]
    groups = tb // LANES
    first = pl.BlockSpec((PEER_HEADS, groups, N_KEYS, LANES), lambda i: (0, i, 0, 0))
    return pl.pallas_call(
        functools.partial(_scores_kernel, tb=tb),
        grid=(t // tb,),
        in_specs=[pl.BlockSpec((D_MODEL, tb), lambda i: (0, i)),
                  pl.BlockSpec(wq_t_bf.shape, lambda i: (0, 0)),
                  pl.BlockSpec(keys_bf.shape, lambda i: (0, 0, 0))],
        out_specs=[first, first, pl.BlockSpec((groups, SECOND_ROWS, LANES), lambda i: (i, 0, 0))],
        out_shape=[jax.ShapeDtypeStruct((PEER_HEADS, t // LANES, N_KEYS, LANES), F32)] * 2
                  + [jax.ShapeDtypeStruct((t // LANES, SECOND_ROWS, LANES), F32)],
        scratch_shapes=[pltpu.VMEM((2, 2, N_KEYS, tb), F32)],
        compiler_params=_params("parallel"),
        name="peer_scores",
    )(xn, wq_t_bf, keys_bf)


CHUNK_TOKENS = 256
SUB_EXPERTS = SUBLANES * N_KEYS


def _experts_kernel(xn_ref, h_ref, u_ref, vt_ref, u_next_ref, vt_prev_ref, cnt_ref, e1_ref, sec_ref, y_ref,
                    hu_ref, w_ref, acc_ref, *, tb, eb):
    e = pl.program_id(1)
    last_step = pl.num_programs(1) - 1
    chunks = tb // CHUNK_TOKENS
    per_chunk = CHUNK_TOKENS // LANES
    stages = [(s, c) for s in range(eb // SUB_EXPERTS) for c in range(chunks)]
    n_stages = len(stages)

    def project(k, src_ref):
        s, c = stages[k]
        xc = xn_ref[:, c * CHUNK_TOKENS:(c + 1) * CHUNK_TOKENS]
        hu = jnp.dot(src_ref[s * SUB_EXPERTS:(s + 1) * SUB_EXPERTS, :], xc, preferred_element_type=F32)
        for j in range(per_chunk):
            hu_ref[k % 2, j] = hu[:, j * LANES:(j + 1) * LANES]

    @pl.when(e == 0)
    def _():
        acc_ref[...] = jnp.zeros_like(acc_ref)
        w_ref[(n_stages - 1) % 2] = jnp.zeros(w_ref.shape[1:], w_ref.dtype)
        project(0, u_ref)

    def weigh(k):
        s, c = stages[k]
        for j in range(per_chunk):
            grp = c * per_chunk + j
            for i1 in range(SUBLANES):
                first = s * SUBLANES + i1
                row = lambda ref, h: jnp.broadcast_to(ref[h, grp, first:first + 1, :], (SUBLANES, LANES))
                cnt = [row(cnt_ref, h) for h in range(PEER_HEADS)]
                e1 = [row(e1_ref, h) for h in range(PEER_HEADS)]
                for r in range(ROW_BLOCKS):
                    gate = [jnp.zeros((SUBLANES, LANES), F32) for _ in range(2)]
                    for h in range(PEER_HEADS):
                        base = (r * PEER_HEADS + h) * PAIR_ROWS
                        for half in range(2):
                            lo = base + half * SUBLANES
                            rank = sec_ref[grp, lo:lo + SUBLANES, :]
                            e2 = sec_ref[grp, lo + ROWS_BF16:lo + ROWS_BF16 + SUBLANES, :]
                            gate[half] = gate[half] + jnp.where(rank < cnt[h], e1[h] * e2, 0.0)
                    erow = slice(i1 * N_KEYS + r * ROWS_BF16, i1 * N_KEYS + (r + 1) * ROWS_BF16)
                    x = hu_ref[k % 2, j, erow, :]
                    act = x * (1.0 + lax.erf(x * np.float32(2.0 ** -0.5)))
                    w_ref[k % 2, erow, j * LANES:(j + 1) * LANES] = (act * jnp.concatenate(gate, axis=0)).astype(BF16)

    def collect(k, src_ref):
        s, c = stages[k]
        acc_ref[c] += jnp.dot(src_ref[:, s * SUB_EXPERTS:(s + 1) * SUB_EXPERTS], w_ref[k % 2],
                              preferred_element_type=F32)

    for k in range(n_stages):
        if k + 1 < n_stages:
            project(k + 1, u_ref)
        else:
            project(0, u_next_ref)
        if k > 0:
            collect(k - 1, vt_ref)
        else:
            collect(n_stages - 1, vt_prev_ref)
        weigh(k)

    @pl.when(e == last_step)
    def _():
        collect(n_stages - 1, vt_ref)
        for c in range(chunks):
            rows = slice(c * CHUNK_TOKENS, (c + 1) * CHUNK_TOKENS)
            y_ref[rows, :] = h_ref[rows, :] + acc_ref[c].T


def _experts(xn, h, u_bf, vt_bf, cnt, e1, sec, tb, eb):
    t = h.shape[0]
    groups = tb // LANES
    tok = lambda **kw: pl.BlockSpec((tb, D_MODEL), lambda i, j: (i, 0), **kw)
    once = dict(pipeline_mode=pl.Buffered(1))
    first = pl.BlockSpec((PEER_HEADS, groups, eb // N_KEYS, LANES), lambda i, j: (0, i, j, 0))
    steps = N_EXPERTS // eb
    assert eb == SUB_EXPERTS and (tb // CHUNK_TOKENS) % 2 == 0
    return pl.pallas_call(
        functools.partial(_experts_kernel, tb=tb, eb=eb),
        grid=(t // tb, steps),
        in_specs=[pl.BlockSpec((D_MODEL, tb), lambda i, j: (0, i)), tok(**once),
                  pl.BlockSpec((eb, D_MODEL), lambda i, j: (j, 0)),
                  pl.BlockSpec((D_MODEL, eb), lambda i, j: (0, j)),
                  pl.BlockSpec((eb, D_MODEL), lambda i, j: (jnp.minimum(j + 1, steps - 1), 0)),
                  pl.BlockSpec((D_MODEL, eb), lambda i, j: (0, jnp.maximum(j - 1, 0))),
                  first, first, pl.BlockSpec((groups, SECOND_ROWS, LANES), lambda i, j: (i, 0, 0), **once)],
        out_specs=tok(),
        out_shape=jax.ShapeDtypeStruct((t, D_MODEL), F32),
        scratch_shapes=[pltpu.VMEM((2, CHUNK_TOKENS // LANES, SUB_EXPERTS, LANES), F32),
                        pltpu.VMEM((2, SUB_EXPERTS, CHUNK_TOKENS), BF16),
                        pltpu.VMEM((tb // CHUNK_TOKENS, D_MODEL, CHUNK_TOKENS), F32)],
        compiler_params=_params("parallel", "arbitrary"),
        name="peer_experts",
    )(xn, h, u_bf, vt_bf, u_bf, vt_bf, cnt, e1, sec)


def _q_perm():
    n = np.arange(ATTN_DIM)
    g, j, d = n // LANES, (n % LANES) // HEAD_DIM, n % HEAD_DIM
    return j * (GQA_GROUP * HEAD_DIM) + g * HEAD_DIM + d


def _segment_matrix():
    i = np.arange(QK_DIM)
    return (i[:, None] // HEAD_DIM == i[None, :] // HEAD_DIM).astype(np.float32)


def _sink_rows(sinks, s_rows):
    s = sinks.astype(F32).reshape(N_KV_HEADS, GQA_GROUP, 1, 1)
    return jnp.broadcast_to(s, (N_KV_HEADS, GQA_GROUP, s_rows, 1)).reshape(N_KV_HEADS, GQA_GROUP * s_rows, 1)


def _layer(l, xp, xs, cache_k, cache_v, state_conv, p):
    (norm1_g, w_in, q_norm_g, k_norm_g, attn_sinks, conv_w, conv_b, conv_ln_g, conv_ln_b, conv_pw_w,
     attn_out_g, conv_out_g, w_out, norm2_g, peer_wq, peer_keys, peer_u, peer_v) = [a[l] for a in p]
    bsz, seq, _ = xp.shape
    dbs, dseq, _ = xs.shape
    perm = _q_perm()

    w_in_bf = jnp.concatenate([w_in[:, :ATTN_DIM][:, perm], w_in[:, ATTN_DIM:]], axis=1).astype(BF16)
    seg = jnp.asarray(_segment_matrix(), BF16)
    gqk = jnp.concatenate([jnp.tile(q_norm_g, N_KV_HEADS * GQA_GROUP) * (HEAD_DIM ** -0.5),
                           jnp.tile(k_norm_g, N_KV_HEADS)])[None, :]
    g1 = norm1_g[None, :]
    ga = attn_out_g[perm][None, :]
    gc = conv_out_g[None, :]
    wa_bf = w_out[:ATTN_DIM][perm].astype(BF16)
    wc_bf = w_out[ATTN_DIM:].astype(BF16)
    g2 = norm2_g[None, :]
    cw = jnp.pad(conv_w, ((0, 1), (0, 0)))
    cb, lg, lb = conv_b[None, :], conv_ln_g[None, :], conv_ln_b[None, :]
    pw_bf = conv_pw_w.astype(BF16)
    wq_t_bf = peer_wq.T.astype(BF16)
    keys_bf = peer_keys.reshape(2 * PEER_HEADS, N_KEYS, PEER_HALF).astype(BF16)
    u_bf = peer_u.astype(BF16)
    vt_bf = peer_v.astype(BF16).T

    def peer(xn, h):
        cnt, e1, sec = _scores(xn, wq_t_bf, keys_bf, SCORE_TOKENS)
        return _experts(xn, h, u_bf, vt_bf, cnt, e1, sec, EXPERT_TOKENS, SUB_EXPERTS)

    tp = bsz * seq
    xpf = xp.reshape(tp, D_MODEL)
    q, k, v, glu = _proj(xpf, g1, w_in_bf, seg, gqk, ROW_TILE_TOKENS)
    q3, k3, v3 = q.reshape(bsz, seq, ATTN_DIM), k.reshape(bsz, seq, KV_DIM), v.reshape(bsz, seq, KV_DIM)
    a = _attn_prompt(q3, k3, v3, _sink_rows(attn_sinks, WINDOW), ATTN_BLOCKS).reshape(tp, ATTN_DIM)
    tile = ROW_TILE_TOKENS
    glu4 = glu.reshape(bsz, seq // tile, tile, CONV_CH)
    hist = jnp.concatenate([jnp.zeros((bsz, 1, HIST_ROWS, CONV_CH), F32), glu4[:, :-1, -HIST_ROWS:]], axis=1)
    c = _conv(hist.reshape(-1, HIST_ROWS, CONV_CH), glu4.reshape(-1, tile, CONV_CH), cw, cb, lg, lb, pw_bf, 1, CONV_ROWS)
    h, xn = _merge(xpf, a, c, ga, gc, wa_bf, wc_bf, g2, ROW_TILE_TOKENS)
    yp = peer(xn, h).reshape(bsz, seq, D_MODEL)
    pk = k3[:, -WINDOW:].reshape(bsz, WINDOW, N_KV_HEADS, HEAD_DIM)
    pv = v3[:, -WINDOW:].reshape(bsz, WINDOW, N_KV_HEADS, HEAD_DIM)
    pc = glu.reshape(bsz, seq, CONV_CH)[:, -(CONV_WIDTH - 1):]

    ts = dbs * dseq
    xsf = xs.reshape(ts, D_MODEL)
    q, k, v, glu = _proj(xsf, g1, w_in_bf, seg, gqk, ROW_TILE_TOKENS)
    q3, k3, v3 = q.reshape(dbs, dseq, ATTN_DIM), k.reshape(dbs, dseq, KV_DIM), v.reshape(dbs, dseq, KV_DIM)
    ck = cache_k.reshape(dbs, -1, KV_DIM)
    cv = cache_v.reshape(dbs, -1, KV_DIM)
    a = _attn_sample(q3, ck, k3, cv, v3, _sink_rows(attn_sinks, dseq), ATTN_SEQS).reshape(ts, ATTN_DIM)
    glu3 = glu.reshape(dbs, dseq, CONV_CH)
    hist = jnp.pad(state_conv, ((0, 0), (HIST_SKIP, 0), (0, 0)))
    c = _conv(hist, glu3, cw, cb, lg, lb, pw_bf, CONV_SEQS, dseq)
    h, xn = _merge(xsf, a, c, ga, gc, wa_bf, wc_bf, g2, ROW_TILE_TOKENS)
    ys = peer(xn, h).reshape(dbs, dseq, D_MODEL)
    wb = ck.shape[1]
    sk = jnp.concatenate([ck, k3], axis=1)[:, -wb:].reshape(dbs, wb, N_KV_HEADS, HEAD_DIM)
    sv = jnp.concatenate([cv, v3], axis=1)[:, -wb:].reshape(dbs, wb, N_KV_HEADS, HEAD_DIM)
    sc = jnp.concatenate([state_conv, glu3], axis=1)[:, -(CONV_WIDTH - 1):]
    return yp, ys, (pk, pv, pc, sk, sv, sc)


def kernel(x_prompt, x_sample, cache_k, cache_v, state_conv, norm1_g, w_in, q_norm_g, k_norm_g, attn_sinks,
           conv_w, conv_b, conv_ln_g, conv_ln_b, conv_pw_w, attn_out_g, conv_out_g, w_out, norm2_g, peer_wq,
           peer_keys, peer_u, peer_v):
    params = (norm1_g, w_in, q_norm_g, k_norm_g, attn_sinks, conv_w, conv_b, conv_ln_g, conv_ln_b, conv_pw_w,
              attn_out_g, conv_out_g, w_out, norm2_g, peer_wq, peer_keys, peer_u, peer_v)
    depth = w_in.shape[0]
    xp, xs = x_prompt, x_sample
    states = []
    for l in range(depth):
        xp, xs, st = _layer(l, xp, xs, cache_k[l], cache_v[l], state_conv[l], params)
        states.append(st)
    stacked = [jnp.stack([st[i] for st in states]) for i in range(6)]
    return (xp, xs, *stacked)
```

```python
import functools

import numpy as np
import jax
import jax.numpy as jnp
from jax import lax
from jax.experimental import pallas as pl
from jax.experimental.pallas import tpu as pltpu

F32 = jnp.float32
BF16 = jnp.bfloat16

D_MODEL = 1024
ATTN_DIM = 512
CONV_CH = 512
HEAD_DIM = 64
N_KV_HEADS = 2
GQA_GROUP = 4
KV_DIM = N_KV_HEADS * HEAD_DIM
WINDOW = 128
CONV_WIDTH = 31
IN_DIM = ATTN_DIM + 2 * KV_DIM + 2 * CONV_CH
QK_DIM = ATTN_DIM + KV_DIM
PEER_HEADS = 8
N_KEYS = 128
N_EXPERTS = N_KEYS * N_KEYS
PEER_TOPK = 16
PEER_HALF = 128
EPS = 1e-6
NEG = -1e30

SUBLANES = 8
LANES = 128
HIST_ROWS = 32
HIST_SKIP = HIST_ROWS - (CONV_WIDTH - 1)

VMEM_LIMIT = 62 * 1024 * 1024

ROW_TILE_TOKENS = 512
SCORE_TOKENS = 512
EXPERT_TOKENS = 1024
ATTN_BLOCKS = 8
ATTN_SEQS = 32
CONV_ROWS = 32
CONV_SEQS = 64


def _params(*sem, **kw):
    return pltpu.CompilerParams(dimension_semantics=sem, vmem_limit_bytes=VMEM_LIMIT, **kw)


def _proj_kernel(x_ref, g1_ref, w_ref, seg_ref, gqk_ref, q_ref, k_ref, v_ref, glu_ref):
    x = x_ref[...]
    xn = x * lax.rsqrt(jnp.mean(x * x, axis=-1, keepdims=True) + EPS) * g1_ref[...]
    z = jnp.dot(xn.astype(BF16), w_ref[...], preferred_element_type=F32)
    qk = z[:, :QK_DIM]
    sq = qk * qk
    hi = sq.astype(BF16)
    lo = (sq - hi.astype(F32)).astype(BF16)
    seg = seg_ref[...]
    ssum = jnp.dot(hi, seg, preferred_element_type=F32) + jnp.dot(lo, seg, preferred_element_type=F32)
    qkn = qk * lax.rsqrt(ssum * (1.0 / HEAD_DIM) + EPS) * gqk_ref[...]
    q_ref[...] = qkn[:, :ATTN_DIM]
    k_ref[...] = qkn[:, ATTN_DIM:]
    v_ref[...] = z[:, QK_DIM:QK_DIM + KV_DIM]
    val = z[:, QK_DIM + KV_DIM:QK_DIM + KV_DIM + CONV_CH]
    gate = z[:, QK_DIM + KV_DIM + CONV_CH:]
    glu_ref[...] = val * jax.nn.sigmoid(gate)


def _proj(x, g1, w_in_bf, seg, gqk, tm):
    t = x.shape[0]
    row = lambda n: pl.BlockSpec((tm, n), lambda i: (i, 0))
    full = lambda a: pl.BlockSpec(a.shape, lambda i: (0,) * a.ndim)
    return pl.pallas_call(
        _proj_kernel,
        grid=(t // tm,),
        in_specs=[row(D_MODEL), full(g1), full(w_in_bf), full(seg), full(gqk)],
        out_specs=[row(ATTN_DIM), row(KV_DIM), row(KV_DIM), row(CONV_CH)],
        out_shape=[jax.ShapeDtypeStruct((t, n), F32) for n in (ATTN_DIM, KV_DIM, KV_DIM, CONV_CH)],
        compiler_params=_params("parallel"),
        name="proj",
    )(x, g1, w_in_bf, seg, gqk)


def _attend(q, kk, vv, sink_ref, first):
    s_rows = q.shape[0]
    n_keys = WINDOW + s_rows
    qs = jnp.concatenate([q[:, c * LANES:(c + 1) * LANES] for c in range(GQA_GROUP)], axis=0)
    lane = lax.broadcasted_iota(jnp.int32, (1, LANES), 1)
    low = lane < HEAD_DIM
    qi = lax.broadcasted_iota(jnp.int32, (s_rows, n_keys), 0)
    kr = lax.broadcasted_iota(jnp.int32, (s_rows, n_keys), 1)
    vis = (kr > qi) & (kr <= qi + WINDOW)
    if first is not None:
        vis = vis & ((kr >= WINDOW) | jnp.logical_not(first))
    vis = jnp.concatenate([vis] * GQA_GROUP, axis=0)
    outs = []
    for j in range(N_KV_HEADS):
        qm = jnp.where(low if j == 0 else jnp.logical_not(low), qs, 0.0).astype(BF16)
        s = lax.dot_general(qm, kk, (((1,), (1,)), ((), ())), preferred_element_type=F32)
        s = jnp.where(vis, s, NEG)
        sink = sink_ref[j]
        m = jnp.maximum(jnp.max(s, axis=-1, keepdims=True), sink)
        p = jnp.exp(s - m)
        denom = jnp.sum(p, axis=-1, keepdims=True) + jnp.exp(sink - m)
        o = jnp.dot(p.astype(BF16), vv, preferred_element_type=F32)
        outs.append(o / denom)
    o = jnp.where(low, outs[0], outs[1])
    return [o[c * s_rows:(c + 1) * s_rows] for c in range(GQA_GROUP)]


def _attn_prompt_kernel(q_ref, kp_ref, kc_ref, vp_ref, vc_ref, sink_ref, a_ref, *, blocks):
    first = pl.program_id(1) == 0
    kk = jnp.concatenate([kp_ref[...], kc_ref[...]], axis=0).astype(BF16)
    vv = jnp.concatenate([vp_ref[...], vc_ref[...]], axis=0).astype(BF16)
    for b in range(blocks):
        rows = slice(b * WINDOW, (b + 1) * WINDOW)
        keys = slice(b * WINDOW, (b + 2) * WINDOW)
        cols = _attend(q_ref[rows, :], kk[keys], vv[keys], sink_ref, first if b == 0 else None)
        for c in range(GQA_GROUP):
            a_ref[rows, c * LANES:(c + 1) * LANES] = cols[c]


def _attn_prompt(q, k, v, sink_rows, blocks):
    b, t, _ = q.shape
    rows = blocks * WINDOW
    cur = lambda n: pl.BlockSpec((None, rows, n), lambda i, j: (i, j, 0))
    prev = lambda n: pl.BlockSpec((None, WINDOW, n), lambda i, j: (i, jnp.maximum(j * blocks - 1, 0), 0))
    return pl.pallas_call(
        functools.partial(_attn_prompt_kernel, blocks=blocks),
        grid=(b, t // rows),
        in_specs=[cur(ATTN_DIM), prev(KV_DIM), cur(KV_DIM), prev(KV_DIM), cur(KV_DIM),
                  pl.BlockSpec(sink_rows.shape, lambda i, j: (0, 0, 0))],
        out_specs=cur(ATTN_DIM),
        out_shape=jax.ShapeDtypeStruct((b, t, ATTN_DIM), F32),
        compiler_params=_params("parallel", "parallel"),
        name="attn_prompt",
    )(q, k, k, v, v, sink_rows)


def _attn_sample_kernel(q_ref, kp_ref, kc_ref, vp_ref, vc_ref, sink_ref, a_ref, *, seqs):
    together = 4

    def one(n, carry):
        for u in range(together):
            i = n * together + u
            kk = jnp.concatenate([kp_ref[i], kc_ref[i]], axis=0).astype(BF16)
            vv = jnp.concatenate([vp_ref[i], vc_ref[i]], axis=0).astype(BF16)
            cols = _attend(q_ref[i], kk, vv, sink_ref, None)
            a_ref[i] = jnp.concatenate(cols, axis=1)
        return carry
    lax.fori_loop(0, seqs // together, one, 0)


def _attn_sample(q, ck, k, cv, v, sink_rows, seqs):
    b, s, _ = q.shape
    blk = lambda r, n: pl.BlockSpec((seqs, r, n), lambda i: (i, 0, 0))
    return pl.pallas_call(
        functools.partial(_attn_sample_kernel, seqs=seqs),
        grid=(b // seqs,),
        in_specs=[blk(s, ATTN_DIM), blk(WINDOW, KV_DIM), blk(s, KV_DIM), blk(WINDOW, KV_DIM), blk(s, KV_DIM),
                  pl.BlockSpec(sink_rows.shape, lambda i: (0, 0, 0))],
        out_specs=blk(s, ATTN_DIM),
        out_shape=jax.ShapeDtypeStruct((b, s, ATTN_DIM), F32),
        compiler_params=_params("parallel"),
        name="attn_sample",
    )(q, ck, k, cv, v, sink_rows)


def _conv_kernel(hist_ref, cur_ref, w_ref, b_ref, lg_ref, lb_ref, pw_ref, out_ref, sh_ref, acc_ref,
                 *, seqs, rows, chunk):
    span = rows + HIST_ROWS - SUBLANES

    def one(i, carry):
        sh_ref[0, 0:HIST_ROWS, :] = hist_ref[i]
        sh_ref[0, HIST_ROWS:HIST_ROWS + rows, :] = cur_ref[i]
        for s in range(1, SUBLANES):
            sh_ref[s, 0:span, :] = sh_ref[0, s:s + span, :]
        base = pl.multiple_of(i * rows, SUBLANES)
        for c in range(rows // chunk):
            acc = jnp.broadcast_to(b_ref[...], (chunk, CONV_CH))
            for j in range(CONV_WIDTH):
                off = HIST_SKIP + j
                lo = c * chunk + off - off % SUBLANES
                acc = acc + sh_ref[off % SUBLANES, lo:lo + chunk, :] * w_ref[j:j + 1, :]
            acc_ref[pl.ds(base + c * chunk, chunk), :] = acc
        return carry
    lax.fori_loop(0, seqs, one, 0)
    c = acc_ref[...]
    mu = jnp.mean(c, axis=-1, keepdims=True)
    d = c - mu
    var = jnp.mean(d * d, axis=-1, keepdims=True)
    y = d * lax.rsqrt(var + EPS) * lg_ref[...] + lb_ref[...]
    y = y * jax.nn.sigmoid(y)
    out_ref[...] = jnp.dot(y.astype(BF16), pw_ref[...], preferred_element_type=F32)


def _conv(hist, cur, w, b, lg, lb, pw_bf, seqs, chunk):
    n, rows, _ = cur.shape
    full = lambda a: pl.BlockSpec(a.shape, lambda i: (0,) * a.ndim)
    return pl.pallas_call(
        functools.partial(_conv_kernel, seqs=seqs, rows=rows, chunk=chunk),
        grid=(n // seqs,),
        in_specs=[pl.BlockSpec((seqs, HIST_ROWS, CONV_CH), lambda i: (i, 0, 0)),
                  pl.BlockSpec((seqs, rows, CONV_CH), lambda i: (i, 0, 0)),
                  full(w), full(b), full(lg), full(lb), full(pw_bf)],
        out_specs=pl.BlockSpec((seqs * rows, CONV_CH), lambda i: (i, 0)),
        out_shape=jax.ShapeDtypeStruct((n * rows, CONV_CH), F32),
        scratch_shapes=[pltpu.VMEM((SUBLANES, HIST_ROWS + rows, CONV_CH), F32),
                        pltpu.VMEM((seqs * rows, CONV_CH), F32)],
        compiler_params=_params("parallel"),
        name="conv",
    )(hist, cur, w, b, lg, lb, pw_bf)


def _rms(x, g):
    return x * lax.rsqrt(jnp.mean(x * x, axis=-1, keepdims=True) + EPS) * g


def _merge_kernel(x_ref, a_ref, c_ref, ga_ref, gc_ref, wa_ref, wc_ref, g2_ref, h_ref, xn_ref):
    an = _rms(a_ref[...], ga_ref[...]).astype(BF16)
    cn = _rms(c_ref[...], gc_ref[...]).astype(BF16)
    h = (x_ref[...] + jnp.dot(an, wa_ref[...], preferred_element_type=F32)
         + jnp.dot(cn, wc_ref[...], preferred_element_type=F32))
    h_ref[...] = h
    xn_ref[...] = _rms(h, g2_ref[...]).T.astype(BF16)


def _merge(x, a, c, ga, gc, wa_bf, wc_bf, g2, tm):
    t = x.shape[0]
    row = lambda n: pl.BlockSpec((tm, n), lambda i: (i, 0))
    full = lambda a_: pl.BlockSpec(a_.shape, lambda i: (0,) * a_.ndim)
    return pl.pallas_call(
        _merge_kernel,
        grid=(t // tm,),
        in_specs=[row(D_MODEL), row(ATTN_DIM), row(CONV_CH), full(ga), full(gc), full(wa_bf), full(wc_bf), full(g2)],
        out_specs=[row(D_MODEL), pl.BlockSpec((D_MODEL, tm), lambda i: (0, i))],
        out_shape=[jax.ShapeDtypeStruct((t, D_MODEL), F32), jax.ShapeDtypeStruct((D_MODEL, t), BF16)],
        compiler_params=_params("parallel"),
        name="merge",
    )(x, a, c, ga, gc, wa_bf, wc_bf, g2)


def _oddeven_merge_sort_pairs(n):
    pairs = []
    p = 1
    while p < n:
        k = p
        while k >= 1:
            for j in range(k % p, n - k, 2 * k):
                for i in range(min(k, n - j - k)):
                    if (i + j) // (2 * p) == (i + j + k) // (2 * p):
                        pairs.append((i + j, i + j + k))
            k //= 2
        p *= 2
    return pairs


_SORT16 = _oddeven_merge_sort_pairs(PEER_TOPK)


def _sort_desc(v):
    v = list(v)
    for i, j in _SORT16:
        v[i], v[j] = jnp.maximum(v[i], v[j]), jnp.minimum(v[i], v[j])
    return v


def _bitonic_to_desc(z):
    z = list(z)
    d = PEER_TOPK // 2
    while d >= 1:
        for i in range(PEER_TOPK):
            if i & d == 0:
                z[i], z[i + d] = jnp.maximum(z[i], z[i + d]), jnp.minimum(z[i], z[i + d])
        d //= 2
    return z


def _merge_top(x, y):
    return _bitonic_to_desc([jnp.maximum(x[i], y[PEER_TOPK - 1 - i]) for i in range(PEER_TOPK)])


def _across_sublanes(x):
    for shift in (4, 2, 1):
        x = _merge_top(x, [pltpu.roll(t, shift, 0) for t in x])
    return x


def _top16(tile):
    v = [tile[i * SUBLANES:(i + 1) * SUBLANES, :] for i in range(N_KEYS // SUBLANES)]
    return _across_sublanes(_sort_desc(v))


def _best_sums(a, b):
    sub = lax.broadcasted_iota(jnp.int32, (SUBLANES, LANES), 0)
    a_lo, a_hi = a[0], a[SUBLANES]
    for r in range(1, SUBLANES):
        a_lo = jnp.where(sub == r, a[r], a_lo)
        a_hi = jnp.where(sub == r, a[SUBLANES + r], a_hi)
    x = [jnp.where(sub < PEER_TOPK // (j + 1), a_lo + b[j], -jnp.inf) for j in range(PEER_TOPK)]
    x[PEER_TOPK - 1] = jnp.maximum(x[PEER_TOPK - 1], a_hi + b[0])
    return _across_sublanes(_bitonic_to_desc(x))


def _first_key_cuts(a, b, thr):
    cuts = []
    for j in range(PEER_TOPK):
        cut = jnp.full_like(thr, jnp.inf)
        for k in range(PEER_TOPK // (j + 1)):
            cut = jnp.where(a[k] + b[j] >= thr, a[k], cut)
        cuts.append(cut)
    return cuts


ROWS_BF16 = 2 * SUBLANES
ROW_BLOCKS = N_KEYS // ROWS_BF16
PAIR_ROWS = 2 * ROWS_BF16
SECOND_ROWS = ROW_BLOCKS * PEER_HEADS * PAIR_ROWS


def _scores_kernel(xn_ref, wq_ref, keys_ref, cnt_ref, e1_ref, sec_ref, s_ref, *, tb):
    groups = tb // LANES
    tiles = N_KEYS // SUBLANES

    def head_scores(h, slot):
        for p in range(2):
            rows = pl.ds(pl.multiple_of((2 * h + p) * PEER_HALF, PEER_HALF), PEER_HALF)
            qt = jnp.dot(wq_ref[rows, :], xn_ref[...], preferred_element_type=F32)
            s_ref[slot, p] = jnp.dot(keys_ref[2 * h + p], qt.astype(BF16), preferred_element_type=F32)

    def one(h, slot, grp):
        lanes = slice(grp * LANES, (grp + 1) * LANES)
        s1 = s_ref[slot, 0, :, lanes]
        s2 = s_ref[slot, 1, :, lanes]
        a = _top16(s1)
        b = _top16(s2)
        c = _best_sums(a, b)
        z = jnp.ones_like(c[0])
        for k in range(1, PEER_TOPK):
            z = z + jnp.exp(c[k] - c[0])
        inv_z = 0.5 / z
        cuts = _first_key_cuts(a, b, c[PEER_TOPK - 1])
        cnt, e1, rank, e2 = [], [], [], []
        for i in range(tiles):
            x1 = s1[i * SUBLANES:(i + 1) * SUBLANES, :]
            x2 = s2[i * SUBLANES:(i + 1) * SUBLANES, :]
            n = jnp.zeros_like(x1)
            r = jnp.zeros_like(x2)
            for k in range(PEER_TOPK):
                n = jnp.where(x1 >= cuts[k], k + 1.0, n)
                r = jnp.where(b[k] > x2, k + 1.0, r)
            cnt.append(n)
            rank.append(r)
            e1.append(jnp.exp(x1 - a[0]) * inv_z)
            e2.append(jnp.exp(x2 - b[0]))
        cnt_ref[h, grp] = jnp.concatenate(cnt, axis=0)
        e1_ref[h, grp] = jnp.concatenate(e1, axis=0)
        for r in range(ROW_BLOCKS):
            row = pl.multiple_of((r * PEER_HEADS + h) * PAIR_ROWS, PAIR_ROWS)
            pair = jnp.concatenate(rank[2 * r:2 * r + 2] + e2[2 * r:2 * r + 2], axis=0)
            sec_ref[grp, pl.ds(row, PAIR_ROWS), :] = pair

    def per_head_pair(i, carry):
        for slot in range(2):
            h = 2 * i + slot
            head_scores(jnp.minimum(h + 1, PEER_HEADS - 1), 1 - slot)
            for grp in range(groups):
                one(h, slot, grp)
        return carry

    head_scores(jnp.int32(0), 0)
    lax.fori_loop(0, PEER_HEADS // 2, per_head_pair, 0)


def _scores(xn, wq_t_bf, keys_bf, tb):
    t = xn.shape[1]
    groups = tb // LANES
    first = pl.BlockSpec((PEER_HEADS, groups, N_KEYS, LANES), lambda i: (0, i, 0, 0))
    return pl.pallas_call(
        functools.partial(_scores_kernel, tb=tb),
        grid=(t // tb,),
        in_specs=[pl.BlockSpec((D_MODEL, tb), lambda i: (0, i)),
                  pl.BlockSpec(wq_t_bf.shape, lambda i: (0, 0)),
                  pl.BlockSpec(keys_bf.shape, lambda i: (0, 0, 0))],
        out_specs=[first, first, pl.BlockSpec((groups, SECOND_ROWS, LANES), lambda i: (i, 0, 0))],
        out_shape=[jax.ShapeDtypeStruct((PEER_HEADS, t // LANES, N_KEYS, LANES), F32)] * 2
                  + [jax.ShapeDtypeStruct((t // LANES, SECOND_ROWS, LANES), F32)],
        scratch_shapes=[pltpu.VMEM((2, 2, N_KEYS, tb), F32)],
        compiler_params=_params("parallel"),
        name="peer_scores",
    )(xn, wq_t_bf, keys_bf)


CHUNK_TOKENS = 256
SUB_EXPERTS = SUBLANES * N_KEYS


def _experts_kernel(xn_ref, h_ref, u_ref, vt_ref, u_next_ref, vt_prev_ref, cnt_ref, e1_ref, sec_ref, y_ref,
                    hu_ref, w_ref, acc_ref, *, tb, eb):
    e = pl.program_id(1)
    last_step = pl.num_programs(1) - 1
    chunks = tb // CHUNK_TOKENS
    per_chunk = CHUNK_TOKENS // LANES
    stages = [(s, c) for s in range(eb // SUB_EXPERTS) for c in range(chunks)]
    n_stages = len(stages)

    def project(k, src_ref):
        s, c = stages[k]
        xc = xn_ref[:, c * CHUNK_TOKENS:(c + 1) * CHUNK_TOKENS]
        hu = jnp.dot(src_ref[s * SUB_EXPERTS:(s + 1) * SUB_EXPERTS, :], xc, preferred_element_type=F32)
        for j in range(per_chunk):
            hu_ref[k % 2, j] = hu[:, j * LANES:(j + 1) * LANES]

    @pl.when(e == 0)
    def _():
        acc_ref[...] = jnp.zeros_like(acc_ref)
        w_ref[(n_stages - 1) % 2] = jnp.zeros(w_ref.shape[1:], w_ref.dtype)
        project(0, u_ref)

    def weigh(k):
        s, c = stages[k]
        for j in range(per_chunk):
            grp = c * per_chunk + j
            for i1 in range(SUBLANES):
                first = s * SUBLANES + i1
                row = lambda ref, h: jnp.broadcast_to(ref[h, grp, first:first + 1, :], (SUBLANES, LANES))
                cnt = [row(cnt_ref, h) for h in range(PEER_HEADS)]
                e1 = [row(e1_ref, h) for h in range(PEER_HEADS)]
                for r in range(ROW_BLOCKS):
                    gate = [jnp.zeros((SUBLANES, LANES), F32) for _ in range(2)]
                    for h in range(PEER_HEADS):
                        base = (r * PEER_HEADS + h) * PAIR_ROWS
                        for half in range(2):
                            lo = base + half * SUBLANES
                            rank = sec_ref[grp, lo:lo + SUBLANES, :]
                            e2 = sec_ref[grp, lo + ROWS_BF16:lo + ROWS_BF16 + SUBLANES, :]
                            gate[half] = gate[half] + jnp.where(rank < cnt[h], e1[h] * e2, 0.0)
                    erow = slice(i1 * N_KEYS + r * ROWS_BF16, i1 * N_KEYS + (r + 1) * ROWS_BF16)
                    x = hu_ref[k % 2, j, erow, :]
                    act = x * (1.0 + lax.erf(x * np.float32(2.0 ** -0.5)))
                    w_ref[k % 2, erow, j * LANES:(j + 1) * LANES] = (act * jnp.concatenate(gate, axis=0)).astype(BF16)

    def collect(k, src_ref):
        s, c = stages[k]
        acc_ref[c] += jnp.dot(src_ref[:, s * SUB_EXPERTS:(s + 1) * SUB_EXPERTS], w_ref[k % 2],
                              preferred_element_type=F32)

    for k in range(n_stages):
        if k + 1 < n_stages:
            project(k + 1, u_ref)
        else:
            project(0, u_next_ref)
        if k > 0:
            collect(k - 1, vt_ref)
        else:
            collect(n_stages - 1, vt_prev_ref)
        weigh(k)

    @pl.when(e == last_step)
    def _():
        collect(n_stages - 1, vt_ref)
        for c in range(chunks):
            rows = slice(c * CHUNK_TOKENS, (c + 1) * CHUNK_TOKENS)
            y_ref[rows, :] = h_ref[rows, :] + acc_ref[c].T


def _experts(xn, h, u_bf, vt_bf, cnt, e1, sec, tb, eb):
    t = h.shape[0]
    groups = tb // LANES
    tok = lambda **kw: pl.BlockSpec((tb, D_MODEL), lambda i, j: (i, 0), **kw)
    once = dict(pipeline_mode=pl.Buffered(1))
    first = pl.BlockSpec((PEER_HEADS, groups, eb // N_KEYS, LANES), lambda i, j: (0, i, j, 0))
    steps = N_EXPERTS // eb
    assert eb == SUB_EXPERTS and (tb // CHUNK_TOKENS) % 2 == 0
    return pl.pallas_call(
        functools.partial(_experts_kernel, tb=tb, eb=eb),
        grid=(t // tb, steps),
        in_specs=[pl.BlockSpec((D_MODEL, tb), lambda i, j: (0, i)), tok(**once),
                  pl.BlockSpec((eb, D_MODEL), lambda i, j: (j, 0)),
                  pl.BlockSpec((D_MODEL, eb), lambda i, j: (0, j)),
                  pl.BlockSpec((eb, D_MODEL), lambda i, j: (jnp.minimum(j + 1, steps - 1), 0)),
                  pl.BlockSpec((D_MODEL, eb), lambda i, j: (0, jnp.maximum(j - 1, 0))),
                  first, first, pl.BlockSpec((groups, SECOND_ROWS, LANES), lambda i, j: (i, 0, 0))],
        out_specs=tok(),
        out_shape=jax.ShapeDtypeStruct((t, D_MODEL), F32),
        scratch_shapes=[pltpu.VMEM((2, CHUNK_TOKENS // LANES, SUB_EXPERTS, LANES), F32),
                        pltpu.VMEM((2, SUB_EXPERTS, CHUNK_TOKENS), BF16),
                        pltpu.VMEM((tb // CHUNK_TOKENS, D_MODEL, CHUNK_TOKENS), F32)],
        compiler_params=_params("parallel", "arbitrary"),
        name="peer_experts",
    )(xn, h, u_bf, vt_bf, u_bf, vt_bf, cnt, e1, sec)


def _q_perm():
    n = np.arange(ATTN_DIM)
    g, j, d = n // LANES, (n % LANES) // HEAD_DIM, n % HEAD_DIM
    return j * (GQA_GROUP * HEAD_DIM) + g * HEAD_DIM + d


def _segment_matrix():
    i = np.arange(QK_DIM)
    return (i[:, None] // HEAD_DIM == i[None, :] // HEAD_DIM).astype(np.float32)


def _sink_rows(sinks, s_rows):
    s = sinks.astype(F32).reshape(N_KV_HEADS, GQA_GROUP, 1, 1)
    return jnp.broadcast_to(s, (N_KV_HEADS, GQA_GROUP, s_rows, 1)).reshape(N_KV_HEADS, GQA_GROUP * s_rows, 1)


def _layer(l, xp, xs, cache_k, cache_v, state_conv, p):
    (norm1_g, w_in, q_norm_g, k_norm_g, attn_sinks, conv_w, conv_b, conv_ln_g, conv_ln_b, conv_pw_w,
     attn_out_g, conv_out_g, w_out, norm2_g, peer_wq, peer_keys, peer_u, peer_v) = [a[l] for a in p]
    bsz, seq, _ = xp.shape
    dbs, dseq, _ = xs.shape
    perm = _q_perm()

    w_in_bf = jnp.concatenate([w_in[:, :ATTN_DIM][:, perm], w_in[:, ATTN_DIM:]], axis=1).astype(BF16)
    seg = jnp.asarray(_segment_matrix(), BF16)
    gqk = jnp.concatenate([jnp.tile(q_norm_g, N_KV_HEADS * GQA_GROUP) * (HEAD_DIM ** -0.5),
                           jnp.tile(k_norm_g, N_KV_HEADS)])[None, :]
    g1 = norm1_g[None, :]
    ga = attn_out_g[perm][None, :]
    gc = conv_out_g[None, :]
    wa_bf = w_out[:ATTN_DIM][perm].astype(BF16)
    wc_bf = w_out[ATTN_DIM:].astype(BF16)
    g2 = norm2_g[None, :]
    cw = jnp.pad(conv_w, ((0, 1), (0, 0)))
    cb, lg, lb = conv_b[None, :], conv_ln_g[None, :], conv_ln_b[None, :]
    pw_bf = conv_pw_w.astype(BF16)
    wq_t_bf = peer_wq.T.astype(BF16)
    keys_bf = peer_keys.reshape(2 * PEER_HEADS, N_KEYS, PEER_HALF).astype(BF16)
    u_bf = peer_u.astype(BF16)
    vt_bf = peer_v.astype(BF16).T

    def peer(xn, h):
        cnt, e1, sec = _scores(xn, wq_t_bf, keys_bf, SCORE_TOKENS)
        return _experts(xn, h, u_bf, vt_bf, cnt, e1, sec, EXPERT_TOKENS, SUB_EXPERTS)

    tp = bsz * seq
    xpf = xp.reshape(tp, D_MODEL)
    q, k, v, glu = _proj(xpf, g1, w_in_bf, seg, gqk, ROW_TILE_TOKENS)
    q3, k3, v3 = q.reshape(bsz, seq, ATTN_DIM), k.reshape(bsz, seq, KV_DIM), v.reshape(bsz, seq, KV_DIM)
    a = _attn_prompt(q3, k3, v3, _sink_rows(attn_sinks, WINDOW), ATTN_BLOCKS).reshape(tp, ATTN_DIM)
    tile = ROW_TILE_TOKENS
    glu4 = glu.reshape(bsz, seq // tile, tile, CONV_CH)
    hist = jnp.concatenate([jnp.zeros((bsz, 1, HIST_ROWS, CONV_CH), F32), glu4[:, :-1, -HIST_ROWS:]], axis=1)
    c = _conv(hist.reshape(-1, HIST_ROWS, CONV_CH), glu4.reshape(-1, tile, CONV_CH), cw, cb, lg, lb, pw_bf, 1, CONV_ROWS)
    h, xn = _merge(xpf, a, c, ga, gc, wa_bf, wc_bf, g2, ROW_TILE_TOKENS)
    yp = peer(xn, h).reshape(bsz, seq, D_MODEL)
    pk = k3[:, -WINDOW:].reshape(bsz, WINDOW, N_KV_HEADS, HEAD_DIM)
    pv = v3[:, -WINDOW:].reshape(bsz, WINDOW, N_KV_HEADS, HEAD_DIM)
    pc = glu.reshape(bsz, seq, CONV_CH)[:, -(CONV_WIDTH - 1):]

    ts = dbs * dseq
    xsf = xs.reshape(ts, D_MODEL)
    q, k, v, glu = _proj(xsf, g1, w_in_bf, seg, gqk, ROW_TILE_TOKENS)
    q3, k3, v3 = q.reshape(dbs, dseq, ATTN_DIM), k.reshape(dbs, dseq, KV_DIM), v.reshape(dbs, dseq, KV_DIM)
    ck = cache_k.reshape(dbs, -1, KV_DIM)
    cv = cache_v.reshape(dbs, -1, KV_DIM)
    a = _attn_sample(q3, ck, k3, cv, v3, _sink_rows(attn_sinks, dseq), ATTN_SEQS).reshape(ts, ATTN_DIM)
    glu3 = glu.reshape(dbs, dseq, CONV_CH)
    hist = jnp.pad(state_conv, ((0, 0), (HIST_SKIP, 0), (0, 0)))
    c = _conv(hist, glu3, cw, cb, lg, lb, pw_bf, CONV_SEQS, dseq)
    h, xn = _merge(xsf, a, c, ga, gc, wa_bf, wc_bf, g2, ROW_TILE_TOKENS)
    ys = peer(xn, h).reshape(dbs, dseq, D_MODEL)
    wb = ck.shape[1]
    sk = jnp.concatenate([ck, k3], axis=1)[:, -wb:].reshape(dbs, wb, N_KV_HEADS, HEAD_DIM)
    sv = jnp.concatenate([cv, v3], axis=1)[:, -wb:].reshape(dbs, wb, N_KV_HEADS, HEAD_DIM)
    sc = jnp.concatenate([state_conv, glu3], axis=1)[:, -(CONV_WIDTH - 1):]
    return yp, ys, (pk, pv, pc, sk, sv, sc)


def kernel(x_prompt, x_sample, cache_k, cache_v, state_conv, norm1_g, w_in, q_norm_g, k_norm_g, attn_sinks,
           conv_w, conv_b, conv_ln_g, conv_ln_b, conv_pw_w, attn_out_g, conv_out_g, w_out, norm2_g, peer_wq,
           peer_keys, peer_u, peer_v):
    params = (norm1_g, w_in, q_norm_g, k_norm_g, attn_sinks, conv_w, conv_b, conv_ln_g, conv_ln_b, conv_pw_w,
              attn_out_g, conv_out_g, w_out, norm2_g, peer_wq, peer_keys, peer_u, peer_v)
    depth = w_in.shape[0]
    xp, xs = x_prompt, x_sample
    states = []
    for l in range(depth):
        xp, xs, st = _layer(l, xp, xs, cache_k[l], cache_v[l], state_conv[l], params)
        states.append(st)
    stacked = [jnp.stack([st[i] for st in states]) for i in range(6)]
    return (xp, xs, *stacked)
```
